```python
import jax, jax.numpy as jnp
from jax import lax
import numpy as np

D_MODEL = 1024
BATCH = 2
SEQ = 8192
DEPTH = 1
DEC_BATCH = 128
DEC_SEQ = 1
PAST_LEN = 8192
PAGE_SIZE = 128

A_HEADS = 8
A_HEAD_DIM = 64
A_WIDTH = A_HEADS * A_HEAD_DIM
MOBA_BLOCK = 256
MOBA_TOPK = 3
M_HEADS = 8
M_NOPE = 64
M_ROPE = 32
M_VDIM = 64
M_WIDTH = M_HEADS * M_VDIM
Q_LORA = 256
KV_LORA = 128
ROPE_THETA = 10000.0
MIX_WIDTH = A_WIDTH + M_WIDTH
IN_COLS = 3 * A_WIDTH + Q_LORA + KV_LORA + M_ROPE
D_FF = ((8 * D_MODEL // 3 + 255) // 256) * 256
Q_BLOCK = 128
EPS = 1e-6

kernel_name = 'moba_mla_hybrid_step'


def _rmsnorm(x, g):
    xf = x.astype(jnp.float32)
    y = xf * lax.rsqrt(jnp.mean(xf * xf, axis=-1, keepdims=True) + EPS)
    return (y * g.astype(jnp.float32)).astype(x.dtype)


def _alibi_slopes(n):
    return 2.0 ** (-8.0 * jnp.arange(1, n + 1, dtype=jnp.float32) / n)


def _rope(x, pos):
    half = x.shape[-1] // 2
    inv = ROPE_THETA ** (-jnp.arange(half, dtype=jnp.float32) / half)
    ang = pos.astype(jnp.float32)[:, None] * inv[None, :]
    cos = jnp.cos(ang)[None, :, None, :]
    sin = jnp.sin(ang)[None, :, None, :]
    xf = x.astype(jnp.float32)
    x1, x2 = xf[..., :half], xf[..., half:]
    return jnp.concatenate([x1 * cos - x2 * sin, x2 * cos + x1 * sin], axis=-1).astype(x.dtype)


def _alibi_softmax(scores, qpos, kpos, valid, slopes):
    dist = (qpos - kpos).astype(jnp.float32)
    logits = jnp.where(valid, scores - slopes * dist, -jnp.inf)
    return jax.nn.softmax(logits, axis=-1)


def _project(h, pos, g_attn, w_in, g_q, w_q_up, g_kv):
    B, T, _ = h.shape
    z = _rmsnorm(h, g_attn) @ w_in
    cuts = [A_WIDTH, 2 * A_WIDTH, 3 * A_WIDTH, 3 * A_WIDTH + Q_LORA, 3 * A_WIDTH + Q_LORA + KV_LORA]
    qa_, ka_, va_, qd, kvd, kr = jnp.split(z, cuts, axis=-1)
    q_a = qa_.reshape(B, T, A_HEADS, A_HEAD_DIM)
    k_a = ka_.reshape(B, T, A_HEADS, A_HEAD_DIM)
    v_a = va_.reshape(B, T, A_HEADS, A_HEAD_DIM)
    q = (_rmsnorm(qd, g_q) @ w_q_up).reshape(B, T, M_HEADS, M_NOPE + M_ROPE)
    q_nope = q[..., :M_NOPE]
    q_pe = _rope(q[..., M_NOPE:], pos)
    ckv = _rmsnorm(kvd, g_kv)
    k_pe = _rope(kr[:, :, None, :], pos)[:, :, 0]
    return q_a, k_a, v_a, q_nope, q_pe, ckv, k_pe


def _moba_prompt(q, k, v, slopes):
    B, S, H, Dh = q.shape
    nb = -(-S // MOBA_BLOCK)
    pad = nb * MOBA_BLOCK - S
    kb = jnp.pad(k, ((0, 0), (0, pad), (0, 0), (0, 0))).reshape(B, nb, MOBA_BLOCK, H, Dh)
    vb = jnp.pad(v, ((0, 0), (0, pad), (0, 0), (0, 0))).reshape(B, nb, MOBA_BLOCK, H, Dh)
    kmean = jnp.sum(kb, axis=2, dtype=jnp.float32) / MOBA_BLOCK
    kbh = kb.transpose(0, 3, 1, 2, 4)
    vbh = vb.transpose(0, 3, 1, 2, 4)
    n_sel = min(MOBA_TOPK, nb)
    nsk = n_sel * MOBA_BLOCK
    scale = Dh ** -0.5
    bi = jnp.arange(B)[:, None, None, None]
    hi = jnp.arange(H)[None, None, :, None]
    blk_ids = jnp.arange(nb)
    offs = jnp.arange(MOBA_BLOCK)
    sl = slopes[None, None, :, None]

    def one_block(c):
        start = c * Q_BLOCK
        qc = lax.dynamic_slice_in_dim(q, start, Q_BLOCK, axis=1)
        qpos = start + jnp.arange(Q_BLOCK)
        qblk = qpos // MOBA_BLOCK
        gate = jnp.einsum('bthd,bnhd->bthn', qc.astype(jnp.float32), kmean)
        gate = jnp.where(blk_ids[None, None, None, :] < qblk[None, :, None, None], gate, -jnp.inf)
        _, sel = lax.top_k(gate, n_sel)
        sel_ok = sel < qblk[None, :, None, None]
        ks = kbh[bi, hi, sel].reshape(B, Q_BLOCK, H, nsk, Dh)
        vs = vbh[bi, hi, sel].reshape(B, Q_BLOCK, H, nsk, Dh)
        spos = (sel[..., None] * MOBA_BLOCK + offs).reshape(B, Q_BLOCK, H, nsk)
        sok = jnp.repeat(sel_ok, MOBA_BLOCK, axis=-1)
        own = start // MOBA_BLOCK
        ko = lax.dynamic_index_in_dim(kb, own, axis=1, keepdims=False)
        vo = lax.dynamic_index_in_dim(vb, own, axis=1, keepdims=False)
        opos = own * MOBA_BLOCK + offs
        s = jnp.concatenate([jnp.einsum('bthd,bthnd->bthn', qc, ks),
                             jnp.einsum('bthd,bshd->bths', qc, ko)], axis=-1).astype(jnp.float32) * scale
        kpos = jnp.concatenate([spos, jnp.broadcast_to(opos, (B, Q_BLOCK, H, MOBA_BLOCK))], axis=-1)
        own_ok = jnp.broadcast_to(opos[None, None, None, :] <= qpos[None, :, None, None], (B, Q_BLOCK, H, MOBA_BLOCK))
        valid = jnp.concatenate([sok, own_ok], axis=-1)
        p = _alibi_softmax(s, qpos[None, :, None, None], kpos, valid, sl).astype(v.dtype)
        return (jnp.einsum('bthn,bthnd->bthd', p[..., :nsk], vs)
                + jnp.einsum('bths,bshd->bthd', p[..., nsk:], vo))

    out = lax.map(one_block, jnp.arange(S // Q_BLOCK))
    return out.transpose(1, 0, 2, 3, 4).reshape(B, S, H, Dh)


def _gather_rows(cache, new, page_table, pos, past_len):
    Bd, T, H, _ = new.shape
    bi = jnp.arange(Bd)[:, None, None, None]
    hi = jnp.arange(H)[None, None, :, None]
    ppos = jnp.clip(pos, 0, past_len - 1)
    page = page_table[bi, ppos // PAGE_SIZE]
    from_past = cache[page, ppos % PAGE_SIZE, hi]
    from_new = new[bi, jnp.clip(pos - past_len, 0, T - 1), hi]
    return jnp.where((pos < past_len)[..., None], from_past, from_new)


def _moba_sample(q, k_new, v_new, cache_k, cache_v, page_table, slopes):
    Bd, T, H, Dh = q.shape
    n_pages = page_table.shape[1]
    past_len = n_pages * PAGE_SIZE
    nb = -(-(past_len + T) // MOBA_BLOCK)
    ppb = MOBA_BLOCK // PAGE_SIZE
    page_sums = lax.map(lambda pt: jnp.sum(cache_k[pt], axis=1, dtype=jnp.float32), page_table)
    page_sums = jnp.pad(page_sums, ((0, 0), (0, nb * ppb - n_pages), (0, 0), (0, 0)))
    blk_sums = page_sums.reshape(Bd, nb, ppb, H, Dh).sum(axis=2)
    new_pos = past_len + jnp.arange(T)
    onehot = (new_pos[:, None] // MOBA_BLOCK == jnp.arange(nb)[None, :]).astype(jnp.float32)
    blk_sums = blk_sums + jnp.einsum('tn,bthd->bnhd', onehot, k_new.astype(jnp.float32))
    kmean = blk_sums / MOBA_BLOCK
    qblk = new_pos // MOBA_BLOCK
    gate = jnp.einsum('bthd,bnhd->bthn', q.astype(jnp.float32), kmean)
    gate = jnp.where(jnp.arange(nb)[None, None, None, :] < qblk[None, :, None, None], gate, -jnp.inf)
    n_sel = min(MOBA_TOPK, nb)
    nsk = n_sel * MOBA_BLOCK
    _, sel = lax.top_k(gate, n_sel)
    sok = jnp.repeat(sel < qblk[None, :, None, None], MOBA_BLOCK, axis=-1)
    offs = jnp.arange(MOBA_BLOCK)
    spos = (sel[..., None] * MOBA_BLOCK + offs).reshape(Bd, T, H, nsk)
    opos = (qblk * MOBA_BLOCK)[:, None] + offs[None, :]
    opos_b = jnp.broadcast_to(opos[None, :, None, :], (Bd, T, H, MOBA_BLOCK))
    kpos = jnp.concatenate([spos, opos_b], axis=-1)
    valid = jnp.concatenate([sok, opos_b <= new_pos[None, :, None, None]], axis=-1)
    ks = _gather_rows(cache_k, k_new, page_table, kpos, past_len)
    vs = _gather_rows(cache_v, v_new, page_table, kpos, past_len)
    s = jnp.einsum('bthd,bthnd->bthn', q, ks).astype(jnp.float32) * (Dh ** -0.5)
    p = _alibi_softmax(s, new_pos[None, :, None, None], kpos, valid, slopes[None, None, :, None]).astype(vs.dtype)
    return jnp.einsum('bthn,bthnd->bthd', p, vs)


def _split_kv_up(w_kv_up):
    w = w_kv_up.reshape(KV_LORA, M_HEADS, M_NOPE + M_VDIM)
    return w[..., :M_NOPE], w[..., M_NOPE:]


def _mla_prompt(q_nope, q_pe, ckv, k_pe, w_kv_up):
    B, S, H, _ = q_nope.shape
    w_uk, w_uv = _split_kv_up(w_kv_up)
    k_nope = jnp.einsum('bsc,chd->bshd', ckv, w_uk)
    v = jnp.einsum('bsc,chd->bshd', ckv, w_uv)
    scale = (M_NOPE + M_ROPE) ** -0.5
    kpos = jnp.arange(S)

    def one_block(c):
        start = c * Q_BLOCK
        qn = lax.dynamic_slice_in_dim(q_nope, start, Q_BLOCK, axis=1)
        qp = lax.dynamic_slice_in_dim(q_pe, start, Q_BLOCK, axis=1)
        qpos = start + jnp.arange(Q_BLOCK)
        s = (jnp.einsum('bthd,bshd->bhts', qn, k_nope)
             + jnp.einsum('bthr,bsr->bhts', qp, k_pe)).astype(jnp.float32) * scale
        s = jnp.where(kpos[None, None, None, :] <= qpos[None, None, :, None], s, -jnp.inf)
        p = jax.nn.softmax(s, axis=-1).astype(v.dtype)
        return jnp.einsum('bhts,bshd->bthd', p, v)

    out = lax.map(one_block, jnp.arange(S // Q_BLOCK))
    return out.transpose(1, 0, 2, 3, 4).reshape(B, S, H, M_VDIM)


def _mla_sample(q_nope, q_pe, ckv, k_pe, cache_ckv, cache_kpe, page_table, w_kv_up):
    Bd, T, H, _ = q_nope.shape
    past_len = page_table.shape[1] * PAGE_SIZE
    w_uk, w_uv = _split_kv_up(w_kv_up)
    pc = cache_ckv[page_table].reshape(Bd, past_len, KV_LORA)
    pr = cache_kpe[page_table].reshape(Bd, past_len, M_ROPE)
    q_lat = jnp.einsum('bthd,chd->bthc', q_nope, w_uk)
    s_past = jnp.einsum('bthc,bsc->bths', q_lat, pc) + jnp.einsum('bthr,bsr->bths', q_pe, pr)
    s_new = jnp.einsum('bthc,buc->bthu', q_lat, ckv) + jnp.einsum('bthr,bur->bthu', q_pe, k_pe)
    s = jnp.concatenate([s_past, s_new], axis=-1).astype(jnp.float32) * ((M_NOPE + M_ROPE) ** -0.5)
    causal = jnp.arange(T)[None, :] <= jnp.arange(T)[:, None]
    valid = jnp.concatenate([jnp.ones((T, past_len), dtype=bool), causal], axis=-1)
    s = jnp.where(valid[None, :, None, :], s, -jnp.inf)
    p = jax.nn.softmax(s, axis=-1).astype(ckv.dtype)
    o_lat = (jnp.einsum('bths,bsc->bthc', p[..., :past_len], pc)
             + jnp.einsum('bthu,buc->bthc', p[..., past_len:], ckv))
    return jnp.einsum('bthc,chd->bthd', o_lat, w_uv)


def _finish(h, o_a, o_m, g_out_moba, g_out_mla, w_o, g_ffn, w_gate_up, w_down):
    B, T, _ = h.shape
    o_a = _rmsnorm(o_a.reshape(B, T, A_WIDTH), g_out_moba)
    o_m = _rmsnorm(o_m.reshape(B, T, M_WIDTH), g_out_mla)
    h = h + jnp.concatenate([o_a, o_m], axis=-1) @ w_o
    gu = _rmsnorm(h, g_ffn) @ w_gate_up
    g, u = gu[..., :D_FF], gu[..., D_FF:]
    return h + (jax.nn.silu(g) * u) @ w_down


def setup_inputs(seed: int = 0) -> dict:
    key = jax.random.key(seed)
    ks = jax.random.split(key, 24)
    n_pages = PAST_LEN // PAGE_SIZE
    n_used = DEC_BATCH * n_pages
    n_pool = n_used + (n_used + 3) // 4

    def nrm(k, shape, scale=1.0):
        return jax.random.normal(k, shape, jnp.float32) * scale

    def gain(k, shape):
        return 1.0 + 0.1 * jax.random.normal(k, shape, jnp.float32)

    page_table = jax.random.permutation(ks[0], n_pool)[:n_used].reshape(DEC_BATCH, n_pages).astype(jnp.int32)
    return {
        'x_prompt': nrm(ks[1], (BATCH, SEQ, D_MODEL)),
        'x_sample': nrm(ks[2], (DEC_BATCH, DEC_SEQ, D_MODEL)),
        'cache_moba_k': nrm(ks[3], (DEPTH, n_pool, PAGE_SIZE, A_HEADS, A_HEAD_DIM)),
        'cache_moba_v': nrm(ks[4], (DEPTH, n_pool, PAGE_SIZE, A_HEADS, A_HEAD_DIM)),
        'cache_mla_ckv': nrm(ks[5], (DEPTH, n_pool, PAGE_SIZE, KV_LORA)),
        'cache_mla_kpe': nrm(ks[6], (DEPTH, n_pool, PAGE_SIZE, M_ROPE)),
        'page_table': page_table,
        'g_attn': gain(ks[7], (DEPTH, D_MODEL)),
        'w_in': nrm(ks[8], (DEPTH, D_MODEL, IN_COLS), D_MODEL ** -0.5),
        'g_q': gain(ks[9], (DEPTH, Q_LORA)),
        'w_q_up': nrm(ks[10], (DEPTH, Q_LORA, M_HEADS * (M_NOPE + M_ROPE)), Q_LORA ** -0.5),
        'g_kv': gain(ks[11], (DEPTH, KV_LORA)),
        'w_kv_up': nrm(ks[12], (DEPTH, KV_LORA, M_HEADS * (M_NOPE + M_VDIM)), KV_LORA ** -0.5),
        'g_out_moba': gain(ks[13], (DEPTH, A_WIDTH)),
        'g_out_mla': gain(ks[14], (DEPTH, M_WIDTH)),
        'w_o': nrm(ks[15], (DEPTH, MIX_WIDTH, D_MODEL), MIX_WIDTH ** -0.5),
        'g_ffn': gain(ks[16], (DEPTH, D_MODEL)),
        'w_gate_up': nrm(ks[17], (DEPTH, D_MODEL, 2 * D_FF), D_MODEL ** -0.5),
        'w_down': nrm(ks[18], (DEPTH, D_FF, D_MODEL), D_FF ** -0.5),
        'g_final': gain(ks[19], (D_MODEL,)),
    }


def reference(x_prompt, x_sample, cache_moba_k, cache_moba_v, cache_mla_ckv, cache_mla_kpe, page_table,
              g_attn, w_in, g_q, w_q_up, g_kv, w_kv_up, g_out_moba, g_out_mla, w_o, g_ffn, w_gate_up, w_down,
              g_final):
    slopes = _alibi_slopes(A_HEADS)
    past_len = page_table.shape[1] * PAGE_SIZE
    pos_p = jnp.arange(x_prompt.shape[1])
    pos_s = past_len + jnp.arange(x_sample.shape[1])
    hp, hs = x_prompt, x_sample
    kp_l, vp_l, cp_l, rp_l, ks_l, vs_l, cs_l, rs_l = [], [], [], [], [], [], [], []
    for l in range(DEPTH):
        qa, ka, va, qn, qp, ckv, kpe = _project(hp, pos_p, g_attn[l], w_in[l], g_q[l], w_q_up[l], g_kv[l])
        oa = _moba_prompt(qa, ka, va, slopes)
        om = _mla_prompt(qn, qp, ckv, kpe, w_kv_up[l])
        hp = _finish(hp, oa, om, g_out_moba[l], g_out_mla[l], w_o[l], g_ffn[l], w_gate_up[l], w_down[l])
        kp_l.append(ka); vp_l.append(va); cp_l.append(ckv); rp_l.append(kpe)
        qa, ka, va, qn, qp, ckv, kpe = _project(hs, pos_s, g_attn[l], w_in[l], g_q[l], w_q_up[l], g_kv[l])
        oa = _moba_sample(qa, ka, va, cache_moba_k[l], cache_moba_v[l], page_table, slopes)
        om = _mla_sample(qn, qp, ckv, kpe, cache_mla_ckv[l], cache_mla_kpe[l], page_table, w_kv_up[l])
        hs = _finish(hs, oa, om, g_out_moba[l], g_out_mla[l], w_o[l], g_ffn[l], w_gate_up[l], w_down[l])
        ks_l.append(ka); vs_l.append(va); cs_l.append(ckv); rs_l.append(kpe)
    y_prompt = _rmsnorm(hp, g_final)
    y_sample = _rmsnorm(hs, g_final)
    k_prompt = jnp.stack(kp_l)
    v_prompt = jnp.stack(vp_l)
    ckv_prompt = jnp.stack(cp_l)
    kpe_prompt = jnp.stack(rp_l)
    k_sample = jnp.stack(ks_l)
    v_sample = jnp.stack(vs_l)
    ckv_sample = jnp.stack(cs_l)
    kpe_sample = jnp.stack(rs_l)
    return (y_prompt, y_sample, k_prompt, v_prompt, ckv_prompt, kpe_prompt, k_sample, v_sample, ckv_sample, kpe_sample)
```

```python
import functools

import jax
import jax.numpy as jnp
from jax import lax
from jax.experimental import pallas as pl
from jax.experimental.pallas import tpu as pltpu

F32 = jnp.float32
BF16 = jnp.bfloat16

D_MODEL = 1024
A_HEADS = 8
A_HEAD_DIM = 64
A_WIDTH = A_HEADS * A_HEAD_DIM
MOBA_BLOCK = 256
MOBA_TOPK = 3
M_HEADS = 8
M_NOPE = 64
M_ROPE = 32
M_VDIM = 64
M_WIDTH = M_HEADS * M_VDIM
Q_LORA = 256
KV_LORA = 128
ROPE_THETA = 10000.0
D_FF = 2816
PAGE_SIZE = 128
EPS = 1e-6

LANES = 128
SUBLANES = 8
HEAD_PAIRS = A_HEADS // 2
MQ = 128
NEG = -1e30
VMEM_LIMIT = 56 * 1024 * 1024

C_QA, C_KA, C_VA = 0, A_WIDTH, 2 * A_WIDTH
C_QD = 3 * A_WIDTH
C_KVD = C_QD + Q_LORA
C_KR = C_KVD + KV_LORA
C_KRR = C_KR + LANES
N_W1 = C_KRR + LANES


def _rms(x, g):
    return x * lax.rsqrt(jnp.mean(x * x, axis=-1, keepdims=True) + EPS) * g


def _dot(a, b, precision=None):
    return jnp.dot(a, b, preferred_element_type=F32, precision=precision)


def _dot_nt(a, b, precision=None):
    return lax.dot_general(a, b, (((1,), (1,)), ((), ())), preferred_element_type=F32, precision=precision)


def _proj_kernel(x_ref, cos_ref, sin_ref, ga_ref, w1_ref, gq_ref, wq_ref, gkv_ref, *rest, prompt, tm, qk_scale):
    if prompt:
        (wuk_ref, wuvt_ref, kat_o, vat_o, ckv_o, kpet_o, qaf_o, qm_o, qab_o, kab_o, vatb_o, km_o, vmtb_o,
         ksum_o) = rest
    else:
        wqa_ref, kat_o, vat_o, ckv_o, kpet_o, qaf_o, qm_o, ka_o, va_o, kpe_o, qat_o = rest
    xn = _rms(x_ref[...], ga_ref[...])
    xb = xn.astype(BF16)
    z = _dot(xb, w1_ref[...])
    ka = z[:, C_KA:C_KA + A_WIDTH]
    va = z[:, C_VA:C_VA + A_WIDTH]
    if prompt:
        qa = z[:, C_QA:C_QA + A_WIDTH]
    else:
        qa = _dot(xn, wqa_ref[...], precision=lax.Precision.HIGHEST)
    cos = cos_ref[...]
    sin = sin_ref[...]
    kpe = z[:, C_KR:C_KR + LANES] * cos + z[:, C_KRR:C_KRR + LANES] * sin
    ckv = _rms(z[:, C_KVD:C_KVD + KV_LORA], gkv_ref[...])
    qn = _rms(z[:, C_QD:C_QD + Q_LORA], gq_ref[...]).astype(BF16)
    qab = _dot(qn, wq_ref[...])
    kat = ka.T
    vat = va.T
    kat_o[...] = kat
    vat_o[...] = vat
    ckv_o[...] = ckv
    kpet_o[...] = kpe.T[:M_ROPE]
    qaf_o[...] = qa
    for h in range(M_HEADS):
        sl = slice(h * MQ, (h + 1) * MQ)
        q_h = (qab[:, sl] * cos + qab[:, M_HEADS * MQ + h * MQ:M_HEADS * MQ + (h + 1) * MQ] * sin) * qk_scale
        qm_o[:, sl] = q_h.astype(qm_o.dtype)
    if prompt:
        ckv_b = ckv.astype(BF16)
        qab_o[...] = (qa * (A_HEAD_DIM ** -0.5)).astype(BF16)
        kab_o[...] = ka.astype(BF16)
        kn = _dot(ckv_b, wuk_ref[...])
        for h in range(M_HEADS):
            sl = slice(h * MQ, (h + 1) * MQ)
            km_o[:, sl] = (kn[:, sl] + kpe).astype(BF16)
        vatb = vat.astype(BF16)
        vmtb = _dot_nt(wuvt_ref[...], ckv_b).astype(BF16)
        for j in range(tm // MOBA_BLOCK):
            rows = slice(j * MOBA_BLOCK, (j + 1) * MOBA_BLOCK)
            vatb_o[j] = vatb[:, rows]
            vmtb_o[j] = vmtb[:, rows]
            ksum_o[j] = jnp.sum(ka[rows], axis=0, keepdims=True)
    else:
        ka_o[...] = ka
        va_o[...] = va
        kpe_o[...] = kpe[:, :M_ROPE]
        qat_o[...] = qa.T


def _const_spec(shape):
    return pl.BlockSpec(shape, lambda *_: (0,) * len(shape))


def _project(x2d, cos_tab, sin_tab, g_attn, w1, g_q, wq, g_kv, extra_weights, *, prompt, tm, seq):
    t = x2d.shape[0]
    assert t % tm == 0 and seq % tm == 0
    n_pos_tiles = seq // tm
    batch = t // seq
    row = lambda i: (i, 0)
    tok_minor = lambda i: (i // n_pos_tiles, 0, i % n_pos_tiles)
    in_specs = [
        pl.BlockSpec((tm, D_MODEL), row),
        pl.BlockSpec((tm, LANES), lambda i: (i % n_pos_tiles, 0)),
        pl.BlockSpec((tm, LANES), lambda i: (i % n_pos_tiles, 0)),
        _const_spec((1, D_MODEL)),
        _const_spec((D_MODEL, N_W1)),
        _const_spec((1, Q_LORA)),
        _const_spec((Q_LORA, 2 * M_HEADS * MQ)),
        _const_spec((1, KV_LORA)),
    ]
    args = [x2d, cos_tab, sin_tab, g_attn, w1, g_q, wq, g_kv]
    f32_out = lambda n: jax.ShapeDtypeStruct((t, n), F32)
    tm_out = lambda n: jax.ShapeDtypeStruct((batch, n, seq), F32)
    out_shape = [tm_out(A_WIDTH), tm_out(A_WIDTH), f32_out(KV_LORA), tm_out(M_ROPE), f32_out(A_WIDTH)]
    out_specs = [pl.BlockSpec((None, A_WIDTH, tm), tok_minor), pl.BlockSpec((None, A_WIDTH, tm), tok_minor),
                 pl.BlockSpec((tm, KV_LORA), row), pl.BlockSpec((None, M_ROPE, tm), tok_minor),
                 pl.BlockSpec((tm, A_WIDTH), row)]
    if prompt:
        assert tm % MOBA_BLOCK == 0
        wuk, wuvt = extra_weights
        nb_tile = tm // MOBA_BLOCK
        nblk = seq // MOBA_BLOCK
        in_specs += [_const_spec((KV_LORA, M_HEADS * MQ)), _const_spec((M_WIDTH, KV_LORA))]
        args += [wuk, wuvt]
        bf_out = lambda n: jax.ShapeDtypeStruct((t, n), BF16)
        vt_shape = jax.ShapeDtypeStruct((batch, nblk, A_WIDTH, MOBA_BLOCK), BF16)
        vt_spec = pl.BlockSpec((None, nb_tile, A_WIDTH, MOBA_BLOCK),
                               lambda i: (i // n_pos_tiles, i % n_pos_tiles, 0, 0))
        out_shape += [bf_out(M_HEADS * MQ), bf_out(A_WIDTH), bf_out(A_WIDTH), vt_shape, bf_out(M_HEADS * MQ),
                      vt_shape, jax.ShapeDtypeStruct((t // MOBA_BLOCK, 1, A_WIDTH), F32)]
        out_specs += [pl.BlockSpec((tm, M_HEADS * MQ), row), pl.BlockSpec((tm, A_WIDTH), row),
                      pl.BlockSpec((tm, A_WIDTH), row), vt_spec, pl.BlockSpec((tm, M_HEADS * MQ), row), vt_spec,
                      pl.BlockSpec((nb_tile, 1, A_WIDTH), lambda i: (i, 0, 0))]
    else:
        assert batch == 1
        (wqa,) = extra_weights
        in_specs += [_const_spec((D_MODEL, A_WIDTH))]
        args += [wqa]
        out_shape += [f32_out(M_HEADS * MQ), f32_out(A_WIDTH), f32_out(A_WIDTH), f32_out(M_ROPE),
                      jax.ShapeDtypeStruct((A_WIDTH, t), F32)]
        out_specs += [pl.BlockSpec((tm, M_HEADS * MQ), row), pl.BlockSpec((tm, A_WIDTH), row),
                      pl.BlockSpec((tm, A_WIDTH), row), pl.BlockSpec((tm, M_ROPE), row),
                      pl.BlockSpec((A_WIDTH, tm), lambda i: (0, i))]
    return pl.pallas_call(
        functools.partial(_proj_kernel, prompt=prompt, tm=tm, qk_scale=(M_NOPE + M_ROPE) ** -0.5),
        grid=(t // tm,),
        in_specs=in_specs,
        out_specs=out_specs,
        out_shape=out_shape,
        compiler_params=pltpu.CompilerParams(dimension_semantics=("arbitrary",), vmem_limit_bytes=VMEM_LIMIT),
        name="proj_prompt" if prompt else "proj_sample",
    )(*args)


def _softmax_tile_update(carry, logits, vt_blk):
    m, l, acc = carry
    m_new = jnp.maximum(m, jnp.max(logits, axis=0, keepdims=True))
    alpha = jnp.exp(m - m_new)
    p = jnp.exp(logits - m_new)
    l = alpha * l + jnp.sum(p, axis=0, keepdims=True)
    acc = alpha * acc + _dot(vt_blk, p.astype(BF16))
    return m_new, l, acc


def _key_minus_query(blk):
    return (lax.broadcasted_iota(jnp.int32, (blk, blk), 0) - lax.broadcasted_iota(jnp.int32, (blk, blk), 1))


def _top_blocks(gate, blk_id, n_cand):
    picks = []
    g = gate
    for _ in range(MOBA_TOPK):
        top = jnp.max(g, axis=0, keepdims=True)
        first = jnp.min(jnp.where(g == top, blk_id, n_cand), axis=0, keepdims=True)
        picks.append(first)
        g = jnp.where(blk_id == first, -jnp.inf, g)
    return picks


def _moba_prompt_kernel(slopes_ref, qf_ref, qb_ref, ksum_ref, k_ref, vt_ref, o_ref, selb_ref, *, nblk):
    blk = MOBA_BLOCK
    hp = pl.program_id(1)
    qi = pl.program_id(2)
    diff = _key_minus_query(blk)
    diff_f = diff.astype(F32)
    lane = lax.broadcasted_iota(jnp.int32, (1, LANES), 1)
    kmean = ksum_ref[...] * (1.0 / blk)
    blk_id = lax.broadcasted_iota(jnp.int32, (nblk, blk), 0)
    past = blk_id < qi
    own_rows = pl.ds(pl.multiple_of(qi * blk, blk), blk)
    row_d = lax.broadcasted_iota(jnp.int32, (LANES, 1), 0)
    out_t = jnp.zeros((LANES, blk), F32)
    for hh in range(2):
        head_lanes = (lane >= A_HEAD_DIM) if hh else (lane < A_HEAD_DIM)
        slope = slopes_ref[2 * hp + hh]
        q_h = jnp.where(head_lanes, qb_ref[...], jnp.zeros((), BF16))
        qf_h = jnp.where(head_lanes, qf_ref[...], 0.0)
        gate = _dot_nt(kmean, qf_h, precision=lax.Precision.HIGHEST)
        sel = jnp.zeros((nblk, blk), jnp.bool_)
        for first in _top_blocks(jnp.where(past, gate, -jnp.inf), blk_id, nblk):
            sel = jnp.logical_or(sel, blk_id == first)
        selb_ref[hh] = jnp.where(jnp.logical_and(sel, past), 0.0, NEG)
        bias = slope * diff_f
        s_own = _dot_nt(k_ref[own_rows, :], q_h)
        logits = jnp.where(diff <= 0, s_own + bias, NEG)
        m0 = jnp.max(logits, axis=0, keepdims=True)
        p0 = jnp.exp(logits - m0)
        carry = (m0, jnp.sum(p0, axis=0, keepdims=True), _dot(vt_ref[qi], p0.astype(BF16)))

        def body(j, carry, q_h=q_h, bias=bias, slope=slope, hh=hh):
            rows = pl.ds(pl.multiple_of(j * blk, blk), blk)
            s = _dot_nt(k_ref[rows, :], q_h)
            row_bias = selb_ref[hh, pl.ds(j, 1), :] + slope * ((j - qi) * blk).astype(F32)
            return _softmax_tile_update(carry, s + bias + row_bias, vt_ref[j])

        m, l, acc = lax.fori_loop(0, qi, body, carry)
        head_rows = (row_d >= A_HEAD_DIM) if hh else (row_d < A_HEAD_DIM)
        out_t = jnp.where(head_rows, acc / l, out_t)
    o_ref[...] = out_t.T


def _moba_prompt(slopes, qaf, qab, ksum, kab, vatb, *, batch, seq):
    nblk = seq // MOBA_BLOCK
    blk = MOBA_BLOCK
    q_spec = pl.BlockSpec((blk, LANES), lambda b, hp, qi: (b * nblk + qi, hp))
    return pl.pallas_call(
        functools.partial(_moba_prompt_kernel, nblk=nblk),
        grid=(batch, HEAD_PAIRS, nblk),
        in_specs=[
            pl.BlockSpec(memory_space=pltpu.SMEM),
            q_spec,
            q_spec,
            pl.BlockSpec((None, nblk, LANES), lambda b, hp, qi: (b, 0, hp)),
            pl.BlockSpec((seq, LANES), lambda b, hp, qi: (b, hp)),
            pl.BlockSpec((None, nblk, LANES, blk), lambda b, hp, qi: (b, 0, hp, 0)),
        ],
        out_specs=q_spec,
        out_shape=jax.ShapeDtypeStruct((batch * seq, A_WIDTH), F32),
        scratch_shapes=[pltpu.VMEM((2, nblk, blk), F32)],
        compiler_params=pltpu.CompilerParams(dimension_semantics=("arbitrary",) * 3, vmem_limit_bytes=VMEM_LIMIT),
        name="moba_prompt",
    )(slopes, qaf, qab, ksum, kab, vatb)


def _mla_prompt_kernel(q_ref, k_ref, vt_ref, o_ref):
    blk = MOBA_BLOCK
    qi = pl.program_id(2)
    diff = _key_minus_query(blk)
    own_rows = pl.ds(pl.multiple_of(qi * blk, blk), blk)
    row_d = lax.broadcasted_iota(jnp.int32, (LANES, 1), 0)
    out_t = jnp.zeros((LANES, blk), F32)
    for hh in range(2):
        cols = slice(hh * MQ, (hh + 1) * MQ)
        q_h = q_ref[:, cols]
        logits = jnp.where(diff <= 0, _dot_nt(k_ref[own_rows, cols], q_h), NEG)
        m0 = jnp.max(logits, axis=0, keepdims=True)
        p0 = jnp.exp(logits - m0)
        carry = (m0, jnp.sum(p0, axis=0, keepdims=True), _dot(vt_ref[qi], p0.astype(BF16)))

        def body(j, carry, q_h=q_h, cols=cols):
            rows = pl.ds(pl.multiple_of(j * blk, blk), blk)
            return _softmax_tile_update(carry, _dot_nt(k_ref[rows, cols], q_h), vt_ref[j])

        m, l, acc = lax.fori_loop(0, qi, body, carry)
        head_rows = (row_d >= M_VDIM) if hh else (row_d < M_VDIM)
        out_t = jnp.where(head_rows, acc / l, out_t)
    o_ref[...] = out_t.T


def _mla_prompt(qm, km, vmtb, *, batch, seq):
    nblk = seq // MOBA_BLOCK
    blk = MOBA_BLOCK
    return pl.pallas_call(
        _mla_prompt_kernel,
        grid=(batch, HEAD_PAIRS, nblk),
        in_specs=[
            pl.BlockSpec((blk, 2 * MQ), lambda b, hp, qi: (b * nblk + qi, hp)),
            pl.BlockSpec((seq, 2 * MQ), lambda b, hp, qi: (b, hp)),
            pl.BlockSpec((None, nblk, LANES, blk), lambda b, hp, qi: (b, 0, hp, 0)),
        ],
        out_specs=pl.BlockSpec((blk, LANES), lambda b, hp, qi: (b * nblk + qi, hp)),
        out_shape=jax.ShapeDtypeStruct((batch * seq, M_WIDTH), F32),
        compiler_params=pltpu.CompilerParams(dimension_semantics=("arbitrary",) * 3, vmem_limit_bytes=VMEM_LIMIT),
        name="mla_prompt",
    )(qm, km, vmtb)


def _finish_kernel(x_ref, oa_ref, om_ref, goa_ref, gom_ref, wo_ref, gf_ref, wgu_ref, wd_ref, gfin_ref, y_ref):
    oa = _rms(oa_ref[...], goa_ref[...])
    om = _rms(om_ref[...], gom_ref[...])
    mix = jnp.concatenate([oa, om], axis=-1).astype(BF16)
    h = x_ref[...] + _dot(mix, wo_ref[...])
    hn = _rms(h, gf_ref[...]).astype(BF16)
    gu = _dot(hn, wgu_ref[...])
    g = gu[:, :D_FF]
    u = gu[:, D_FF:]
    act = (g / (1.0 + jnp.exp(-g)) * u).astype(BF16)
    h = h + _dot(act, wd_ref[...])
    y_ref[...] = _rms(h, gfin_ref[...])


def _finish(x2d, oa, om, g_oa, g_om, wo, g_ffn, wgu, wd, g_final, *, tm, name):
    t = x2d.shape[0]
    assert t % tm == 0
    row = lambda i: (i, 0)
    single = pl.Buffered(1)
    wspec = lambda shape: pl.BlockSpec(shape, lambda i: (0, 0), pipeline_mode=single)
    return pl.pallas_call(
        _finish_kernel,
        grid=(t // tm,),
        in_specs=[
            pl.BlockSpec((tm, D_MODEL), row),
            pl.BlockSpec((tm, A_WIDTH), row),
            pl.BlockSpec((tm, M_WIDTH), row),
            _const_spec((1, A_WIDTH)),
            _const_spec((1, M_WIDTH)),
            wspec((A_WIDTH + M_WIDTH, D_MODEL)),
            _const_spec((1, D_MODEL)),
            wspec((D_MODEL, 2 * D_FF)),
            wspec((D_FF, D_MODEL)),
            _const_spec((1, D_MODEL)),
        ],
        out_specs=pl.BlockSpec((tm, D_MODEL), row),
        out_shape=jax.ShapeDtypeStruct((t, D_MODEL), F32),
        compiler_params=pltpu.CompilerParams(dimension_semantics=("arbitrary",), vmem_limit_bytes=VMEM_LIMIT),
        name=name,
    )(x2d, oa, om, g_oa, g_om, wo, g_ffn, wgu, wd, g_final)


PAGES_PER_STEP = 16
PAGES_PER_BLOCK = MOBA_BLOCK // PAGE_SIZE
BLOCKS_PER_STEP = PAGES_PER_STEP // PAGES_PER_BLOCK


def _moba_gate_kernel(pt_ref, *refs, n_steps, nblk):
    pages = refs[:PAGES_PER_STEP]
    qat_ref, q_ref, knew_ref, sel_ref, qb_ref, gate_ref = refs[PAGES_PER_STEP:]
    b = pl.program_id(0)
    s = pl.program_id(1)
    lane = lax.broadcasted_iota(jnp.int32, (1, LANES), 1)

    @pl.when(s == 0)
    def _():
        seq_id = lax.broadcasted_iota(jnp.int32, (1, qat_ref.shape[1]), 1)
        col = jnp.sum(jnp.where(seq_id == b, qat_ref[...], 0.0), axis=1, keepdims=True)
        qb_ref[...] = jnp.broadcast_to(col, qb_ref.shape)

    qb = qb_ref[...]
    rows = []
    for i in range(BLOCKS_PER_STEP):
        prod = pages[2 * i][...] * qb + pages[2 * i + 1][...] * qb
        g_row = jnp.zeros((1, LANES), F32)
        for h in range(A_HEADS):
            g_h = jnp.sum(prod[h * A_HEAD_DIM:(h + 1) * A_HEAD_DIM], keepdims=True)
            g_row = jnp.where(lane == h, g_h, g_row)
        rows.append(g_row)
    gate_ref[pl.ds(pl.multiple_of(s * BLOCKS_PER_STEP, BLOCKS_PER_STEP), BLOCKS_PER_STEP), :] = (
        jnp.concatenate(rows, axis=0) * (1.0 / MOBA_BLOCK))

    @pl.when(s == n_steps - 1)
    def _():
        prod_new = knew_ref[...] * q_ref[...]
        lane_head = lax.broadcasted_iota(jnp.int32, (1, A_WIDTH), 1) // A_HEAD_DIM
        g_new = jnp.zeros((1, LANES), F32)
        for h in range(A_HEADS):
            g_h = jnp.sum(jnp.where(lane_head == h, prod_new, 0.0), axis=1, keepdims=True)
            g_new = jnp.where(lane == h, g_h * (1.0 / MOBA_BLOCK), g_new)
        n_cand = nblk + SUBLANES
        g = jnp.concatenate([gate_ref[...], jnp.broadcast_to(g_new, (SUBLANES, LANES))], axis=0)
        blk_id = lax.broadcasted_iota(jnp.int32, (n_cand, LANES), 0)
        g = jnp.where(blk_id < nblk, g, -jnp.inf)
        picks = _top_blocks(g, blk_id, n_cand)
        sel_ref[...] = jnp.concatenate(picks + [jnp.zeros((SUBLANES - MOBA_TOPK, LANES), jnp.int32)], axis=0)


def _moba_gate(pt_flat, cache_kt, qat, q3, knew3, *, dec_batch, n_pages):
    nblk = n_pages // PAGES_PER_BLOCK
    assert n_pages % PAGES_PER_STEP == 0 and nblk >= MOBA_TOPK and dec_batch <= LANES
    n_steps = n_pages // PAGES_PER_STEP

    def page_spec(i):
        return pl.BlockSpec((None, A_WIDTH, PAGE_SIZE),
                            lambda b, s, pt: (pt[(s * PAGES_PER_STEP + i) * dec_batch + b], 0, 0))

    vec_spec = pl.BlockSpec((None, 1, A_WIDTH), lambda b, s, pt: (b, 0, 0))
    grid_spec = pltpu.PrefetchScalarGridSpec(
        num_scalar_prefetch=1,
        grid=(dec_batch, n_steps),
        in_specs=([page_spec(i) for i in range(PAGES_PER_STEP)]
                  + [pl.BlockSpec((A_WIDTH, dec_batch), lambda b, s, pt: (0, 0)), vec_spec, vec_spec]),
        out_specs=pl.BlockSpec((None, SUBLANES, LANES), lambda b, s, pt: (b, 0, 0)),
        scratch_shapes=[pltpu.VMEM((A_WIDTH, LANES), F32), pltpu.VMEM((nblk, LANES), F32)],
    )
    return pl.pallas_call(
        functools.partial(_moba_gate_kernel, n_steps=n_steps, nblk=nblk),
        grid_spec=grid_spec,
        out_shape=jax.ShapeDtypeStruct((dec_batch, SUBLANES, LANES), jnp.int32),
        compiler_params=pltpu.CompilerParams(dimension_semantics=("arbitrary", "arbitrary"),
                                             vmem_limit_bytes=VMEM_LIMIT),
        name="moba_sample_gate",
    )(pt_flat, *([cache_kt] * PAGES_PER_STEP), qat, q3, knew3)


SLABS = 2 * MOBA_TOPK * PAGES_PER_BLOCK


def _moba_sample_kernel(pt_ref, sel_ref, slopes_ref, *refs, past_len):
    k_refs = refs[:SLABS]
    v_refs = refs[SLABS:2 * SLABS]
    q_ref, knew_ref, vnew_ref, o_ref = refs[2 * SLABS:]
    b = pl.program_id(0)
    hp = pl.program_id(1)
    lane = lax.broadcasted_iota(jnp.int32, (1, PAGE_SIZE), 1)
    scale = A_HEAD_DIM ** -0.5
    outs = []
    for hh in range(2):
        h = 2 * hp + hh
        slope = slopes_ref[h]
        cols = slice(hh * A_HEAD_DIM, (hh + 1) * A_HEAD_DIM)
        q_h = q_ref[:, cols]
        q_rows = jnp.broadcast_to(q_h, (SUBLANES, A_HEAD_DIM)).astype(BF16)
        kts, vts, logits = [], [], []
        for r in range(MOBA_TOPK):
            blk = sel_ref[b * (MOBA_TOPK * A_HEADS) + r * A_HEADS + h]
            for pg in range(PAGES_PER_BLOCK):
                slab = (hh * MOBA_TOPK + r) * PAGES_PER_BLOCK + pg
                s = _dot(q_rows, k_refs[slab][...].astype(BF16))[:1] * scale
                kpos = blk * MOBA_BLOCK + pg * PAGE_SIZE + lane
                logits.append(s - slope * (past_len - kpos).astype(F32))
                vts.append(v_refs[slab][...].astype(BF16))
        s_new = jnp.sum(knew_ref[:, cols] * q_h, axis=1, keepdims=True) * scale
        m = s_new
        for lg in logits:
            m = jnp.maximum(m, jnp.max(lg, axis=1, keepdims=True))
        p_new = jnp.exp(s_new - m)
        l = p_new
        acc = p_new * vnew_ref[:, cols]
        for lg, vt in zip(logits, vts):
            p = jnp.exp(lg - m)
            l = l + jnp.sum(p, axis=1, keepdims=True)
            acc = acc + _dot_nt(jnp.broadcast_to(p, (SUBLANES, PAGE_SIZE)).astype(BF16), vt)[:1]
        outs.append(acc / l)
    o_ref[...] = jnp.concatenate(outs, axis=1)


def _moba_sample(pt_flat, sel_flat, slopes, cache_kt, cache_vt, q3, knew3, vnew3, *, dec_batch, n_pages):
    def slab_spec(slab):
        hh, rem = divmod(slab, MOBA_TOPK * PAGES_PER_BLOCK)
        r, pg = divmod(rem, PAGES_PER_BLOCK)

        def index_map(b, hp, pt, sel):
            blk = sel[b * (MOBA_TOPK * A_HEADS) + r * A_HEADS + 2 * hp + hh]
            return (pt[(blk * PAGES_PER_BLOCK + pg) * dec_batch + b], 2 * hp + hh, 0, 0)

        return pl.BlockSpec((None, None, A_HEAD_DIM, PAGE_SIZE), index_map)

    vec_spec = pl.BlockSpec((None, 1, LANES), lambda b, hp, pt, sel: (b, 0, hp))
    slabs = [slab_spec(i) for i in range(SLABS)]
    grid_spec = pltpu.PrefetchScalarGridSpec(
        num_scalar_prefetch=2,
        grid=(dec_batch, HEAD_PAIRS),
        in_specs=[pl.BlockSpec(memory_space=pltpu.SMEM)] + slabs + slabs + [vec_spec, vec_spec, vec_spec],
        out_specs=vec_spec,
    )
    return pl.pallas_call(
        functools.partial(_moba_sample_kernel, past_len=n_pages * PAGE_SIZE),
        grid_spec=grid_spec,
        out_shape=jax.ShapeDtypeStruct((dec_batch, 1, A_WIDTH), F32),
        compiler_params=pltpu.CompilerParams(dimension_semantics=("arbitrary", "arbitrary"),
                                             vmem_limit_bytes=VMEM_LIMIT),
        name="moba_sample_attn",
    )(pt_flat, sel_flat, slopes, *([cache_kt] * SLABS), *([cache_vt] * SLABS), q3, knew3, vnew3)


def _mla_sample_kernel(pt_ref, *refs, n_steps):
    ckv_pages = refs[:PAGES_PER_STEP]
    kpe_pages = refs[PAGES_PER_STEP:2 * PAGES_PER_STEP]
    q_ref, ckvn_ref, kpen_ref, wuk_ref, wuv_ref, o_ref, qlat_ref, m_ref, l_ref, acc_ref = refs[2 * PAGES_PER_STEP:]
    s = pl.program_id(1)
    q8 = q_ref[...]
    q_pe = q8[:, :M_ROPE]

    @pl.when(s == 0)
    def _():
        head_of_lane = lax.broadcasted_iota(jnp.int32, (M_HEADS, M_HEADS * MQ), 1) // MQ
        head_of_row = lax.broadcasted_iota(jnp.int32, (M_HEADS, M_HEADS * MQ), 0)
        q_bd = jnp.where(head_of_lane == head_of_row, jnp.concatenate([q8] * M_HEADS, axis=1), 0.0)
        qlat_ref[...] = _dot_nt(q_bd.astype(BF16), wuk_ref[...])
        m_ref[...] = jnp.full(m_ref.shape, NEG, F32)
        l_ref[...] = jnp.zeros(l_ref.shape, F32)
        acc_ref[...] = jnp.zeros(acc_ref.shape, F32)

    qlat = qlat_ref[...]
    qlat_b = qlat.astype(BF16)
    qpe_b = q_pe.astype(BF16)
    pcs = [ref[...].astype(BF16) for ref in ckv_pages]
    scores = [_dot_nt(qlat_b, pc) + _dot(qpe_b, kp[...].astype(BF16)) for pc, kp in zip(pcs, kpe_pages)]
    m_old = m_ref[...]
    m_new = m_old
    for sc in scores:
        m_new = jnp.maximum(m_new, jnp.max(sc, axis=1, keepdims=True))
    alpha = jnp.exp(m_old - m_new)
    l = alpha * l_ref[...]
    acc = alpha * acc_ref[...]
    for sc, pc in zip(scores, pcs):
        p = jnp.exp(sc - m_new)
        l = l + jnp.sum(p, axis=1, keepdims=True)
        acc = acc + _dot(p.astype(BF16), pc)
    m_ref[...] = m_new
    l_ref[...] = l
    acc_ref[...] = acc

    @pl.when(s == n_steps - 1)
    def _():
        ckvn = ckvn_ref[...]
        s_new = (jnp.sum(qlat * ckvn, axis=1, keepdims=True)
                 + jnp.sum(q_pe * kpen_ref[...], axis=1, keepdims=True))
        m_fin = jnp.maximum(m_new, s_new)
        a = jnp.exp(m_new - m_fin)
        p_new = jnp.exp(s_new - m_fin)
        o_lat = (a * acc + p_new * ckvn) / (a * l + p_new)
        res = _dot(o_lat.astype(BF16), wuv_ref[...])
        head_of_lane = lax.broadcasted_iota(jnp.int32, (M_HEADS, M_WIDTH), 1) // M_VDIM
        head_of_row = lax.broadcasted_iota(jnp.int32, (M_HEADS, M_WIDTH), 0)
        o_ref[...] = jnp.sum(jnp.where(head_of_lane == head_of_row, res, 0.0), axis=0, keepdims=True)


def _mla_sample(pt_flat, cache_ckv, cache_kpet, q8, ckvn3, kpen3, wuk, wuv, *, dec_batch, n_pages):
    assert n_pages % PAGES_PER_STEP == 0
    n_steps = n_pages // PAGES_PER_STEP

    def page_spec(i, shape):
        return pl.BlockSpec((None,) + shape, lambda b, s, pt: (pt[(s * PAGES_PER_STEP + i) * dec_batch + b], 0, 0))

    per_b = lambda shape: pl.BlockSpec((None,) + shape, lambda b, s, pt: (b, 0, 0))
    grid_spec = pltpu.PrefetchScalarGridSpec(
        num_scalar_prefetch=1,
        grid=(dec_batch, n_steps),
        in_specs=([page_spec(i, (PAGE_SIZE, KV_LORA)) for i in range(PAGES_PER_STEP)]
                  + [page_spec(i, (M_ROPE, PAGE_SIZE)) for i in range(PAGES_PER_STEP)]
                  + [per_b((M_HEADS, MQ)), per_b((1, KV_LORA)), per_b((1, M_ROPE)),
                     pl.BlockSpec((KV_LORA, M_HEADS * MQ), lambda b, s, pt: (0, 0)),
                     pl.BlockSpec((KV_LORA, M_WIDTH), lambda b, s, pt: (0, 0))]),
        out_specs=per_b((1, M_WIDTH)),
        scratch_shapes=[pltpu.VMEM((M_HEADS, KV_LORA), F32), pltpu.VMEM((M_HEADS, 1), F32),
                        pltpu.VMEM((M_HEADS, 1), F32), pltpu.VMEM((M_HEADS, KV_LORA), F32)],
    )
    return pl.pallas_call(
        functools.partial(_mla_sample_kernel, n_steps=n_steps),
        grid_spec=grid_spec,
        out_shape=jax.ShapeDtypeStruct((dec_batch, 1, M_WIDTH), F32),
        compiler_params=pltpu.CompilerParams(dimension_semantics=("arbitrary", "arbitrary"),
                                             vmem_limit_bytes=VMEM_LIMIT),
        name="mla_sample",
    )(pt_flat, *([cache_ckv] * PAGES_PER_STEP), *([cache_kpet] * PAGES_PER_STEP), q8, ckvn3, kpen3, wuk, wuv)


def _rot_half_cols(w):
    half = M_ROPE // 2
    return jnp.concatenate([-w[..., half:], w[..., :half]], axis=-1)


def _prep_weights(w_in, w_q_up, w_kv_up):
    kr = w_in[:, C_KR:C_KR + M_ROPE]
    pad = jnp.zeros((D_MODEL, LANES - M_ROPE), F32)
    w1 = jnp.concatenate([w_in[:, :C_KR], kr, pad, _rot_half_cols(kr), pad], axis=1).astype(BF16)
    wqa = w_in[:, C_QA:C_QA + A_WIDTH]
    wq3 = w_q_up.reshape(Q_LORA, M_HEADS, M_NOPE + M_ROPE)
    nope, rope = wq3[..., :M_NOPE], wq3[..., M_NOPE:]
    z = lambda n: jnp.zeros((Q_LORA, M_HEADS, n), F32)
    wq_a = jnp.concatenate([rope, nope, z(MQ - M_NOPE - M_ROPE)], axis=-1).reshape(Q_LORA, M_HEADS * MQ)
    wq_b = jnp.concatenate([_rot_half_cols(rope), z(MQ - M_ROPE)], axis=-1).reshape(Q_LORA, M_HEADS * MQ)
    wq = jnp.concatenate([wq_a, wq_b], axis=1).astype(BF16)
    wkv3 = w_kv_up.reshape(KV_LORA, M_HEADS, M_NOPE + M_VDIM)
    w_uk, w_uv = wkv3[..., :M_NOPE], wkv3[..., M_NOPE:]
    zk = lambda n: jnp.zeros((KV_LORA, M_HEADS, n), F32)
    wuk = jnp.concatenate([zk(M_ROPE), w_uk, zk(MQ - M_ROPE - M_NOPE)], axis=-1).reshape(KV_LORA, M_HEADS * MQ)
    wuv = w_uv.reshape(KV_LORA, M_WIDTH)
    return w1, wqa, wq, wuk.astype(BF16), wuv.astype(BF16), wuv.T.astype(BF16)


def _rope_tables(pos):
    half = M_ROPE // 2
    inv = ROPE_THETA ** (-jnp.arange(half, dtype=F32) / half)
    ang = pos.astype(F32)[:, None] * inv[None, :]
    n = pos.shape[0]
    cos = jnp.concatenate([jnp.cos(ang), jnp.cos(ang), jnp.ones((n, M_NOPE), F32),
                           jnp.zeros((n, MQ - M_ROPE - M_NOPE), F32)], axis=1)
    sin = jnp.concatenate([jnp.sin(ang), jnp.sin(ang), jnp.zeros((n, MQ - M_ROPE), F32)], axis=1)
    return cos, sin


def kernel(x_prompt, x_sample, cache_moba_k, cache_moba_v, cache_mla_ckv, cache_mla_kpe, page_table, g_attn, w_in,
           g_q, w_q_up, g_kv, w_kv_up, g_out_moba, g_out_mla, w_o, g_ffn, w_gate_up, w_down, g_final):
    batch, seq, _ = x_prompt.shape
    dec_batch, dec_seq, _ = x_sample.shape
    n_pool = cache_moba_k.shape[1]
    n_pages = page_table.shape[1]
    past_len = n_pages * PAGE_SIZE
    assert w_in.shape[0] == 1 and dec_seq == 1 and seq % MOBA_BLOCK == 0 and past_len % MOBA_BLOCK == 0

    w1, wqa, wq, wuk, wuv, wuvt = _prep_weights(w_in[0], w_q_up[0], w_kv_up[0])
    wo = w_o[0].astype(BF16)
    wgu = w_gate_up[0].astype(BF16)
    wd = w_down[0].astype(BF16)
    row2 = lambda a: a.reshape(1, -1)
    slopes = 2.0 ** (-8.0 * jnp.arange(1, A_HEADS + 1, dtype=F32) / A_HEADS)

    cos_p, sin_p = _rope_tables(jnp.arange(seq))
    xp = x_prompt.reshape(batch * seq, D_MODEL)
    (kat, vat, ckv, kpet, qaf, qm, qab, kab, vatb, km, vmtb, ksum) = _project(
        xp, cos_p, sin_p, row2(g_attn[0]), w1, row2(g_q[0]), wq, row2(g_kv[0]), (wuk, wuvt),
        prompt=True, tm=MOBA_BLOCK, seq=seq)
    nblk = seq // MOBA_BLOCK
    oa = _moba_prompt(slopes, qaf, qab, ksum.reshape(batch, nblk, A_WIDTH), kab, vatb, batch=batch, seq=seq)
    om = _mla_prompt(qm, km, vmtb, batch=batch, seq=seq)
    finish_w = (row2(g_out_moba[0]), row2(g_out_mla[0]), wo, row2(g_ffn[0]), wgu, wd, row2(g_final))
    y_prompt = _finish(xp, oa, om, *finish_w, tm=256, name="finish_prompt").reshape(batch, seq, D_MODEL)

    cos_s, sin_s = _rope_tables(jnp.full((dec_batch,), past_len, jnp.int32))
    xs = x_sample.reshape(dec_batch, D_MODEL)
    (kat_s, vat_s, ckv_s, kpet_s, qa_s, qm_s, ka_s, va_s, kpe_s, qat_s) = _project(
        xs, cos_s, sin_s, row2(g_attn[0]), w1, row2(g_q[0]), wq, row2(g_kv[0]), (wqa,),
        prompt=False, tm=dec_batch, seq=dec_batch)
    pt_flat = page_table.T.reshape(-1)
    ckt = jnp.transpose(cache_moba_k[0], (0, 2, 3, 1))
    cvt = jnp.transpose(cache_moba_v[0], (0, 2, 3, 1))
    kpet_cache = jnp.transpose(cache_mla_kpe[0], (0, 2, 1))
    q3 = qa_s.reshape(dec_batch, 1, A_WIDTH)
    knew3 = ka_s.reshape(dec_batch, 1, A_WIDTH)
    vnew3 = va_s.reshape(dec_batch, 1, A_WIDTH)
    sel = _moba_gate(pt_flat, ckt.reshape(n_pool, A_WIDTH, PAGE_SIZE), qat_s, q3, knew3,
                     dec_batch=dec_batch, n_pages=n_pages)
    oa_s = _moba_sample(pt_flat, sel[:, :MOBA_TOPK, :A_HEADS].reshape(-1), slopes, ckt, cvt, q3, knew3, vnew3,
                        dec_batch=dec_batch, n_pages=n_pages)
    om_s = _mla_sample(pt_flat, cache_mla_ckv[0], kpet_cache, qm_s.reshape(dec_batch, M_HEADS, MQ),
                       ckv_s.reshape(dec_batch, 1, KV_LORA), kpe_s.reshape(dec_batch, 1, M_ROPE), wuk, wuv,
                       dec_batch=dec_batch, n_pages=n_pages)
    y_sample = _finish(xs, oa_s.reshape(dec_batch, A_WIDTH), om_s.reshape(dec_batch, M_WIDTH), *finish_w,
                       tm=dec_batch, name="finish_sample").reshape(dec_batch, 1, D_MODEL)

    def heads_last(t_minor, n_tok_batch, n_tok):
        return jnp.transpose(t_minor.reshape(1, n_tok_batch, A_HEADS, A_HEAD_DIM, n_tok), (0, 1, 4, 2, 3))

    k_sample = jnp.transpose(kat_s.reshape(1, 1, A_HEADS, A_HEAD_DIM, dec_batch), (0, 4, 1, 2, 3))
    v_sample = jnp.transpose(vat_s.reshape(1, 1, A_HEADS, A_HEAD_DIM, dec_batch), (0, 4, 1, 2, 3))
    return (y_prompt, y_sample,
            heads_last(kat, batch, seq), heads_last(vat, batch, seq),
            ckv.reshape(1, batch, seq, KV_LORA), jnp.transpose(kpet.reshape(1, batch, M_ROPE, seq), (0, 1, 3, 2)),
            k_sample, v_sample,
            ckv_s.reshape(1, dec_batch, 1, KV_LORA),
            jnp.transpose(kpet_s.reshape(1, 1, M_ROPE, dec_batch), (0, 3, 1, 2)))
```

```python
import functools

import jax
import jax.numpy as jnp
from jax import lax
from jax.experimental import pallas as pl
from jax.experimental.pallas import tpu as pltpu

F32 = jnp.float32
BF16 = jnp.bfloat16

D_MODEL = 1024
A_HEADS = 8
A_HEAD_DIM = 64
A_WIDTH = A_HEADS * A_HEAD_DIM
MOBA_BLOCK = 256
MOBA_TOPK = 3
M_HEADS = 8
M_NOPE = 64
M_ROPE = 32
M_VDIM = 64
M_WIDTH = M_HEADS * M_VDIM
Q_LORA = 256
KV_LORA = 128
ROPE_THETA = 10000.0
D_FF = 2816
PAGE_SIZE = 128
EPS = 1e-6

LANES = 128
SUBLANES = 8
HEAD_PAIRS = A_HEADS // 2
MQ = 128
NEG = -1e30
LOG2E = 1.4426950408889634
VMEM_LIMIT = 56 * 1024 * 1024

C_QA, C_KA, C_VA = 0, A_WIDTH, 2 * A_WIDTH
C_QD = 3 * A_WIDTH
C_KVD = C_QD + Q_LORA
C_KR = C_KVD + KV_LORA
C_KRR = C_KR + LANES
N_W1 = C_KRR + LANES


def _rms(x, g):
    return x * lax.rsqrt(jnp.mean(x * x, axis=-1, keepdims=True) + EPS) * g


def _dot(a, b, precision=None):
    return jnp.dot(a, b, preferred_element_type=F32, precision=precision)


def _dot_nt(a, b, precision=None):
    return lax.dot_general(a, b, (((1,), (1,)), ((), ())), preferred_element_type=F32, precision=precision)


def _proj_kernel(x_ref, cos_ref, sin_ref, ga_ref, w1_ref, gq_ref, wq_ref, gkv_ref, *rest, prompt, tm, qk_scale):
    if prompt:
        (wuk_ref, wuvt_ref, kat_o, vat_o, ckv_o, kpet_o, qaf_o, qm_o, qab_o, kab_o, vatb_o, km_o, vmtb_o,
         ksum_o) = rest
    else:
        wqa_ref, kat_o, vat_o, ckv_o, kpet_o, qaf_o, qm_o, ka_o, va_o, kpe_o, qat_o = rest
    xn = _rms(x_ref[...], ga_ref[...])
    xb = xn.astype(BF16)
    z = _dot(xb, w1_ref[...])
    ka = z[:, C_KA:C_KA + A_WIDTH]
    va = z[:, C_VA:C_VA + A_WIDTH]
    if prompt:
        qa = z[:, C_QA:C_QA + A_WIDTH]
    else:
        qa = _dot(xn, wqa_ref[...], precision=lax.Precision.HIGHEST)
    cos = cos_ref[...]
    sin = sin_ref[...]
    kpe = z[:, C_KR:C_KR + LANES] * cos + z[:, C_KRR:C_KRR + LANES] * sin
    ckv = _rms(z[:, C_KVD:C_KVD + KV_LORA], gkv_ref[...])
    qn = _rms(z[:, C_QD:C_QD + Q_LORA], gq_ref[...]).astype(BF16)
    qab = _dot(qn, wq_ref[...])
    kat = ka.T
    vat = va.T
    kat_o[...] = kat
    vat_o[...] = vat
    ckv_o[...] = ckv
    kpet_o[...] = kpe.T[:M_ROPE]
    qaf_o[...] = qa
    for h in range(M_HEADS):
        sl = slice(h * MQ, (h + 1) * MQ)
        q_h = (qab[:, sl] * cos + qab[:, M_HEADS * MQ + h * MQ:M_HEADS * MQ + (h + 1) * MQ] * sin) * qk_scale
        qm_o[:, sl] = q_h.astype(qm_o.dtype)
    if prompt:
        ckv_b = ckv.astype(BF16)
        qab_o[...] = (qa * (A_HEAD_DIM ** -0.5 * LOG2E)).astype(BF16)
        kab_o[...] = ka.astype(BF16)
        kn = _dot(ckv_b, wuk_ref[...])
        for h in range(M_HEADS):
            sl = slice(h * MQ, (h + 1) * MQ)
            km_o[:, sl] = (kn[:, sl] + kpe).astype(BF16)
        vatb = vat.astype(BF16)
        vmtb = _dot_nt(wuvt_ref[...], ckv_b).astype(BF16)
        for j in range(tm // MOBA_BLOCK):
            rows = slice(j * MOBA_BLOCK, (j + 1) * MOBA_BLOCK)
            vatb_o[j] = vatb[:, rows]
            vmtb_o[j] = vmtb[:, rows]
            ksum_o[j] = jnp.sum(ka[rows], axis=0, keepdims=True)
    else:
        ka_o[...] = ka
        va_o[...] = va
        kpe_o[...] = kpe[:, :M_ROPE]
        qat_o[...] = qa.T


def _const_spec(shape):
    return pl.BlockSpec(shape, lambda *_: (0,) * len(shape))


def _project(x2d, cos_tab, sin_tab, g_attn, w1, g_q, wq, g_kv, extra_weights, *, prompt, tm, seq):
    t = x2d.shape[0]
    assert t % tm == 0 and seq % tm == 0
    n_pos_tiles = seq // tm
    batch = t // seq
    row = lambda i: (i, 0)
    tok_minor = lambda i: (i // n_pos_tiles, 0, i % n_pos_tiles)
    in_specs = [
        pl.BlockSpec((tm, D_MODEL), row),
        pl.BlockSpec((tm, LANES), lambda i: (i % n_pos_tiles, 0)),
        pl.BlockSpec((tm, LANES), lambda i: (i % n_pos_tiles, 0)),
        _const_spec((1, D_MODEL)),
        _const_spec((D_MODEL, N_W1)),
        _const_spec((1, Q_LORA)),
        _const_spec((Q_LORA, 2 * M_HEADS * MQ)),
        _const_spec((1, KV_LORA)),
    ]
    args = [x2d, cos_tab, sin_tab, g_attn, w1, g_q, wq, g_kv]
    f32_out = lambda n: jax.ShapeDtypeStruct((t, n), F32)
    tm_out = lambda n: jax.ShapeDtypeStruct((batch, n, seq), F32)
    out_shape = [tm_out(A_WIDTH), tm_out(A_WIDTH), f32_out(KV_LORA), tm_out(M_ROPE), f32_out(A_WIDTH)]
    out_specs = [pl.BlockSpec((None, A_WIDTH, tm), tok_minor), pl.BlockSpec((None, A_WIDTH, tm), tok_minor),
                 pl.BlockSpec((tm, KV_LORA), row), pl.BlockSpec((None, M_ROPE, tm), tok_minor),
                 pl.BlockSpec((tm, A_WIDTH), row)]
    if prompt:
        assert tm % MOBA_BLOCK == 0
        wuk, wuvt = extra_weights
        nb_tile = tm // MOBA_BLOCK
        nblk = seq // MOBA_BLOCK
        in_specs += [_const_spec((KV_LORA, M_HEADS * MQ)), _const_spec((M_WIDTH, KV_LORA))]
        args += [wuk, wuvt]
        bf_out = lambda n: jax.ShapeDtypeStruct((t, n), BF16)
        vt_shape = jax.ShapeDtypeStruct((batch, nblk, A_WIDTH, MOBA_BLOCK), BF16)
        vt_spec = pl.BlockSpec((None, nb_tile, A_WIDTH, MOBA_BLOCK),
                               lambda i: (i // n_pos_tiles, i % n_pos_tiles, 0, 0))
        out_shape += [bf_out(M_HEADS * MQ), bf_out(A_WIDTH), bf_out(A_WIDTH), vt_shape, bf_out(M_HEADS * MQ),
                      vt_shape, jax.ShapeDtypeStruct((t // MOBA_BLOCK, 1, A_WIDTH), F32)]
        out_specs += [pl.BlockSpec((tm, M_HEADS * MQ), row), pl.BlockSpec((tm, A_WIDTH), row),
                      pl.BlockSpec((tm, A_WIDTH), row), vt_spec, pl.BlockSpec((tm, M_HEADS * MQ), row), vt_spec,
                      pl.BlockSpec((nb_tile, 1, A_WIDTH), lambda i: (i, 0, 0))]
    else:
        assert batch == 1
        (wqa,) = extra_weights
        in_specs += [_const_spec((D_MODEL, A_WIDTH))]
        args += [wqa]
        out_shape += [f32_out(M_HEADS * MQ), f32_out(A_WIDTH), f32_out(A_WIDTH), f32_out(M_ROPE),
                      jax.ShapeDtypeStruct((A_WIDTH, t), F32)]
        out_specs += [pl.BlockSpec((tm, M_HEADS * MQ), row), pl.BlockSpec((tm, A_WIDTH), row),
                      pl.BlockSpec((tm, A_WIDTH), row), pl.BlockSpec((tm, M_ROPE), row),
                      pl.BlockSpec((A_WIDTH, tm), lambda i: (0, i))]
    return pl.pallas_call(
        functools.partial(_proj_kernel, prompt=prompt, tm=tm,
                          qk_scale=(M_NOPE + M_ROPE) ** -0.5 * (LOG2E if prompt else 1.0)),
        grid=(t // tm,),
        in_specs=in_specs,
        out_specs=out_specs,
        out_shape=out_shape,
        compiler_params=pltpu.CompilerParams(dimension_semantics=("arbitrary",), vmem_limit_bytes=VMEM_LIMIT),
        name="proj_prompt" if prompt else "proj_sample",
    )(*args)


ATTN_CHUNK = 2
CHUNKS_PER_STEP = 4


def _run_chains(carry, chains, logits_fn, values_fn):
    blk = MOBA_BLOCK
    carry = list(carry)
    scores = [logits_fn(*chain) for chain in chains]
    for (c, hh, _), logits in zip(chains, scores):
        m, l, acc = carry[hh]
        m_new = jnp.maximum(m, jnp.max(logits, axis=0, keepdims=True))
        alpha = jnp.exp2(m - m_new)
        p = jnp.exp2(logits - m_new)
        l = alpha * l + jnp.sum(p, axis=0, keepdims=True)
        pb = p.astype(BF16)
        vt_parts = values_fn(c, hh)
        pv = _dot(vt_parts[0], pb[:blk])
        for t in range(1, len(vt_parts)):
            pv = pv + _dot(vt_parts[t], pb[t * blk:(t + 1) * blk])
        carry[hh] = (m_new, l, alpha * acc + pv)
    return tuple(carry)


def _attend_causal(qi, chunk, dv, logits_fn, values_fn):
    blk = MOBA_BLOCK
    heads = range(2)
    c_diag = qi // chunk
    n_past_steps = c_diag // CHUNKS_PER_STEP

    def step_chains(i, n_chunks, mask_last):
        return [(i * CHUNKS_PER_STEP + u, hh, mask_last and u == n_chunks - 1)
                for u in range(n_chunks) for hh in heads]

    init = tuple((jnp.full((1, blk), NEG, F32), jnp.zeros((1, blk), F32), jnp.zeros((dv, blk), F32)) for _ in heads)
    last_step = [functools.partial(_run_chains, chains=step_chains(n_past_steps, n, True),
                                   logits_fn=logits_fn, values_fn=values_fn)
                 for n in range(1, CHUNKS_PER_STEP + 1)]
    carry = lax.switch(c_diag % CHUNKS_PER_STEP, last_step, init)
    carry = lax.fori_loop(
        0, n_past_steps,
        lambda i, cr: _run_chains(cr, step_chains(i, CHUNKS_PER_STEP, False), logits_fn, values_fn), carry)
    return jnp.concatenate([acc / l for _, l, acc in carry], axis=0)


def _chunk_key_minus_query(chunk):
    shape = (chunk * MOBA_BLOCK, MOBA_BLOCK)
    return lax.broadcasted_iota(jnp.int32, shape, 0) - lax.broadcasted_iota(jnp.int32, shape, 1)


def _key_minus_query(blk):
    return (lax.broadcasted_iota(jnp.int32, (blk, blk), 0) - lax.broadcasted_iota(jnp.int32, (blk, blk), 1))


def _top_blocks(gate, blk_id, n_cand):
    picks = []
    g = gate
    for _ in range(MOBA_TOPK):
        top = jnp.max(g, axis=0, keepdims=True)
        first = jnp.min(jnp.where(g == top, blk_id, n_cand), axis=0, keepdims=True)
        picks.append(first)
        g = jnp.where(blk_id == first, -jnp.inf, g)
    return picks


def _moba_prompt_kernel(slopes_ref, qf_ref, qb_ref, ksum_ref, k_ref, vt_ref, o_ref, selb_ref, *, nblk, chunk):
    blk = MOBA_BLOCK
    ck = chunk * blk
    hp = pl.program_id(1)
    qi = pl.program_id(2)
    diff_f = _key_minus_query(blk).astype(F32)
    lane = lax.broadcasted_iota(jnp.int32, (1, LANES), 1)
    kmean = ksum_ref[...] * (1.0 / blk)
    blk_id = lax.broadcasted_iota(jnp.int32, (nblk, blk), 0)
    past = blk_id < qi
    heads = range(2)
    q, bias, slope2 = [], [], []
    for hh in heads:
        head_lanes = (lane >= A_HEAD_DIM) if hh else (lane < A_HEAD_DIM)
        q.append(jnp.where(head_lanes, qb_ref[...], jnp.zeros((), BF16)))
        qf_h = jnp.where(head_lanes, qf_ref[...], 0.0)
        gate = _dot_nt(kmean, qf_h, precision=lax.Precision.HIGHEST)
        sel = blk_id == qi
        for first in _top_blocks(jnp.where(past, gate, -jnp.inf), blk_id, nblk):
            sel = jnp.logical_or(sel, jnp.logical_and(blk_id == first, past))
        selb_ref[hh] = jnp.where(sel, 0.0, NEG)
        slope2.append(slopes_ref[2 * hp + hh] * LOG2E)
        bias.append(slope2[hh] * diff_f)

    rel = _chunk_key_minus_query(chunk)

    def logits(c, hh, masked):
        rows = pl.ds(pl.multiple_of(c * ck, ck), ck)
        s = _dot_nt(k_ref[rows, :], q[hh])
        parts = []
        for t in range(chunk):
            j = c * chunk + t
            row_bias = selb_ref[hh, pl.ds(j, 1), :] + slope2[hh] * ((j - qi) * blk).astype(F32)
            parts.append(s[t * blk:(t + 1) * blk] + bias[hh] + row_bias)
        out = jnp.concatenate(parts, axis=0)
        return jnp.where(rel <= (qi - c * chunk) * blk, out, NEG) if masked else out

    def values(c, hh):
        return [vt_ref[c * chunk + t, hh * A_HEAD_DIM:(hh + 1) * A_HEAD_DIM, :] for t in range(chunk)]

    o_ref[...] = _attend_causal(qi, chunk, A_HEAD_DIM, logits, values).T


def _moba_prompt(slopes, qaf, qab, ksum, kab, vatb, *, batch, seq):
    nblk = seq // MOBA_BLOCK
    blk = MOBA_BLOCK
    assert nblk % (ATTN_CHUNK * CHUNKS_PER_STEP) == 0
    q_spec = pl.BlockSpec((blk, LANES), lambda b, hp, qi: (b * nblk + qi, hp))
    return pl.pallas_call(
        functools.partial(_moba_prompt_kernel, nblk=nblk, chunk=ATTN_CHUNK),
        grid=(batch, HEAD_PAIRS, nblk),
        in_specs=[
            pl.BlockSpec(memory_space=pltpu.SMEM),
            q_spec,
            q_spec,
            pl.BlockSpec((None, nblk, LANES), lambda b, hp, qi: (b, 0, hp)),
            pl.BlockSpec((seq, LANES), lambda b, hp, qi: (b, hp)),
            pl.BlockSpec((None, nblk, LANES, blk), lambda b, hp, qi: (b, 0, hp, 0)),
        ],
        out_specs=q_spec,
        out_shape=jax.ShapeDtypeStruct((batch * seq, A_WIDTH), F32),
        scratch_shapes=[pltpu.VMEM((2, nblk, blk), F32)],
        compiler_params=pltpu.CompilerParams(dimension_semantics=("arbitrary",) * 3, vmem_limit_bytes=VMEM_LIMIT),
        name="moba_prompt",
    )(slopes, qaf, qab, ksum, kab, vatb)


def _mla_prompt_kernel(q_ref, k_ref, vt_ref, o_ref, *, chunk):
    blk = MOBA_BLOCK
    ck = chunk * blk
    qi = pl.program_id(2)
    heads = range(2)
    q = [q_ref[:, hh * MQ:(hh + 1) * MQ] for hh in heads]

    rel = _chunk_key_minus_query(chunk)

    def logits(c, hh, masked):
        rows = pl.ds(pl.multiple_of(c * ck, ck), ck)
        s = _dot_nt(k_ref[rows, hh * MQ:(hh + 1) * MQ], q[hh])
        return jnp.where(rel <= (qi - c * chunk) * blk, s, NEG) if masked else s

    def values(c, hh):
        return [vt_ref[c * chunk + t, hh * M_VDIM:(hh + 1) * M_VDIM, :] for t in range(chunk)]

    o_ref[...] = _attend_causal(qi, chunk, M_VDIM, logits, values).T


def _mla_prompt(qm, km, vmtb, *, batch, seq):
    nblk = seq // MOBA_BLOCK
    blk = MOBA_BLOCK
    assert nblk % (ATTN_CHUNK * CHUNKS_PER_STEP) == 0
    return pl.pallas_call(
        functools.partial(_mla_prompt_kernel, chunk=ATTN_CHUNK),
        grid=(batch, HEAD_PAIRS, nblk),
        in_specs=[
            pl.BlockSpec((blk, 2 * MQ), lambda b, hp, qi: (b * nblk + qi, hp)),
            pl.BlockSpec((seq, 2 * MQ), lambda b, hp, qi: (b, hp)),
            pl.BlockSpec((None, nblk, LANES, blk), lambda b, hp, qi: (b, 0, hp, 0)),
        ],
        out_specs=pl.BlockSpec((blk, LANES), lambda b, hp, qi: (b * nblk + qi, hp)),
        out_shape=jax.ShapeDtypeStruct((batch * seq, M_WIDTH), F32),
        compiler_params=pltpu.CompilerParams(dimension_semantics=("arbitrary",) * 3, vmem_limit_bytes=VMEM_LIMIT),
        name="mla_prompt",
    )(qm, km, vmtb)


def _finish_kernel(x_ref, oa_ref, om_ref, goa_ref, gom_ref, wo_ref, gf_ref, wgu_ref, wd_ref, gfin_ref, y_ref):
    oa = _rms(oa_ref[...], goa_ref[...])
    om = _rms(om_ref[...], gom_ref[...])
    mix = jnp.concatenate([oa, om], axis=-1).astype(BF16)
    h = x_ref[...] + _dot(mix, wo_ref[...])
    hn = _rms(h, gf_ref[...]).astype(BF16)
    gu = _dot(hn, wgu_ref[...])
    g = gu[:, :D_FF]
    u = gu[:, D_FF:]
    act = (g / (1.0 + jnp.exp(-g)) * u).astype(BF16)
    h = h + _dot(act, wd_ref[...])
    y_ref[...] = _rms(h, gfin_ref[...])


def _finish(x2d, oa, om, g_oa, g_om, wo, g_ffn, wgu, wd, g_final, *, tm, name):
    t = x2d.shape[0]
    assert t % tm == 0
    row = lambda i: (i, 0)
    single = pl.Buffered(1)
    wspec = lambda shape: pl.BlockSpec(shape, lambda i: (0, 0), pipeline_mode=single)
    return pl.pallas_call(
        _finish_kernel,
        grid=(t // tm,),
        in_specs=[
            pl.BlockSpec((tm, D_MODEL), row),
            pl.BlockSpec((tm, A_WIDTH), row),
            pl.BlockSpec((tm, M_WIDTH), row),
            _const_spec((1, A_WIDTH)),
            _const_spec((1, M_WIDTH)),
            wspec((A_WIDTH + M_WIDTH, D_MODEL)),
            _const_spec((1, D_MODEL)),
            wspec((D_MODEL, 2 * D_FF)),
            wspec((D_FF, D_MODEL)),
            _const_spec((1, D_MODEL)),
        ],
        out_specs=pl.BlockSpec((tm, D_MODEL), row),
        out_shape=jax.ShapeDtypeStruct((t, D_MODEL), F32),
        compiler_params=pltpu.CompilerParams(dimension_semantics=("arbitrary",), vmem_limit_bytes=VMEM_LIMIT),
        name=name,
    )(x2d, oa, om, g_oa, g_om, wo, g_ffn, wgu, wd, g_final)


PAGES_PER_STEP = 16
PAGES_PER_BLOCK = MOBA_BLOCK // PAGE_SIZE
BLOCKS_PER_STEP = PAGES_PER_STEP // PAGES_PER_BLOCK


def _moba_gate_kernel(pt_ref, *refs, n_steps, nblk):
    pages = refs[:PAGES_PER_STEP]
    qat_ref, q_ref, knew_ref, sel_ref, qb_ref, part_ref = refs[PAGES_PER_STEP:]
    b = pl.program_id(0)
    s = pl.program_id(1)
    lane = lax.broadcasted_iota(jnp.int32, (1, LANES), 1)

    @pl.when(s == 0)
    def _():
        seq_id = lax.broadcasted_iota(jnp.int32, (1, qat_ref.shape[1]), 1)
        col = jnp.sum(jnp.where(seq_id == b, qat_ref[...], 0.0), axis=1, keepdims=True)
        qb_ref[...] = jnp.broadcast_to(col, qb_ref.shape)

    qb = qb_ref[...]
    for i in range(BLOCKS_PER_STEP):
        prod = (pages[2 * i][...] + pages[2 * i + 1][...]) * qb
        tile = jnp.concatenate(
            [jnp.sum(prod[h * A_HEAD_DIM:(h + 1) * A_HEAD_DIM], axis=0, keepdims=True) for h in range(A_HEADS)],
            axis=0)
        part_ref[pl.ds(pl.multiple_of((s * BLOCKS_PER_STEP + i) * A_HEADS, A_HEADS), A_HEADS), :] = tile

    @pl.when(s == n_steps - 1)
    def _():
        gate = jnp.sum(part_ref[...], axis=1, keepdims=True) * (1.0 / MOBA_BLOCK)
        prod_new = knew_ref[...] * q_ref[...]
        lane_head = lax.broadcasted_iota(jnp.int32, (1, A_WIDTH), 1) // A_HEAD_DIM
        head_row = lax.broadcasted_iota(jnp.int32, (A_HEADS, 1), 0)
        g_new = jnp.zeros((A_HEADS, 1), F32)
        for h in range(A_HEADS):
            g_h = jnp.sum(jnp.where(lane_head == h, prod_new, 0.0), axis=1, keepdims=True)
            g_new = jnp.where(head_row == h, g_h * (1.0 / MOBA_BLOCK), g_new)
        n_cand = nblk + 1
        cand = [gate[j * A_HEADS:(j + 1) * A_HEADS] for j in range(nblk)] + [g_new]
        cand = [g if j < nblk else jnp.full_like(g, -jnp.inf) for j, g in enumerate(cand)]
        out = jnp.zeros((A_HEADS, LANES), jnp.int32)
        for r in range(MOBA_TOPK):
            top = functools.reduce(jnp.maximum, cand)
            first = functools.reduce(jnp.minimum, [jnp.where(g == top, j, n_cand) for j, g in enumerate(cand)])
            out = jnp.where(lane == r, first, out)
            cand = [jnp.where(first == j, -jnp.inf, g) for j, g in enumerate(cand)]
        sel_ref[...] = out


def _moba_gate(pt_flat, cache_kt, qat, q3, knew3, *, dec_batch, n_pages):
    nblk = n_pages // PAGES_PER_BLOCK
    assert n_pages % PAGES_PER_STEP == 0 and nblk >= MOBA_TOPK and dec_batch <= LANES
    n_steps = n_pages // PAGES_PER_STEP

    def page_spec(i):
        return pl.BlockSpec((None, A_WIDTH, PAGE_SIZE),
                            lambda b, s, pt: (pt[(s * PAGES_PER_STEP + i) * dec_batch + b], 0, 0))

    vec_spec = pl.BlockSpec((None, 1, A_WIDTH), lambda b, s, pt: (b, 0, 0))
    grid_spec = pltpu.PrefetchScalarGridSpec(
        num_scalar_prefetch=1,
        grid=(dec_batch, n_steps),
        in_specs=([page_spec(i) for i in range(PAGES_PER_STEP)]
                  + [pl.BlockSpec((A_WIDTH, dec_batch), lambda b, s, pt: (0, 0)), vec_spec, vec_spec]),
        out_specs=pl.BlockSpec((None, SUBLANES, LANES), lambda b, s, pt: (b, 0, 0)),
        scratch_shapes=[pltpu.VMEM((A_WIDTH, LANES), F32), pltpu.VMEM((nblk * A_HEADS, LANES), F32)],
    )
    return pl.pallas_call(
        functools.partial(_moba_gate_kernel, n_steps=n_steps, nblk=nblk),
        grid_spec=grid_spec,
        out_shape=jax.ShapeDtypeStruct((dec_batch, SUBLANES, LANES), jnp.int32),
        compiler_params=pltpu.CompilerParams(dimension_semantics=("arbitrary", "arbitrary"),
                                             vmem_limit_bytes=VMEM_LIMIT),
        name="moba_sample_gate",
    )(pt_flat, *([cache_kt] * PAGES_PER_STEP), qat, q3, knew3)


SLABS = 2 * MOBA_TOPK * PAGES_PER_BLOCK


def _moba_sample_kernel(pt_ref, sel_ref, slopes_ref, *refs, past_len):
    k_refs = refs[:SLABS]
    v_refs = refs[SLABS:2 * SLABS]
    q_ref, knew_ref, vnew_ref, o_ref = refs[2 * SLABS:]
    b = pl.program_id(0)
    hp = pl.program_id(1)
    n_keys = MOBA_TOPK * MOBA_BLOCK
    slabs_per_head = MOBA_TOPK * PAGES_PER_BLOCK
    lane = lax.broadcasted_iota(jnp.int32, (1, n_keys), 1)
    scale = A_HEAD_DIM ** -0.5
    heads = range(2)
    q, scores = [], []
    for hh in heads:
        q.append(q_ref[:, hh * A_HEAD_DIM:(hh + 1) * A_HEAD_DIM])
        kt = jnp.concatenate([k_refs[hh * slabs_per_head + i][...] for i in range(slabs_per_head)], axis=1)
        q_rows = jnp.broadcast_to(q[hh], (SUBLANES, A_HEAD_DIM)).astype(BF16)
        scores.append(_dot(q_rows, kt.astype(BF16))[:1] * scale)
    outs = []
    for hh in heads:
        h = 2 * hp + hh
        cols = slice(hh * A_HEAD_DIM, (hh + 1) * A_HEAD_DIM)
        kpos = lane % MOBA_BLOCK
        for r in range(MOBA_TOPK):
            blk = sel_ref[(b * A_HEADS + h) * MOBA_TOPK + r]
            kpos = kpos + jnp.where(lane // MOBA_BLOCK == r, blk * MOBA_BLOCK, 0)
        logits = scores[hh] - slopes_ref[h] * (past_len - kpos).astype(F32)
        s_new = jnp.sum(knew_ref[:, cols] * q[hh], axis=1, keepdims=True) * scale
        m = jnp.maximum(s_new, jnp.max(logits, axis=1, keepdims=True))
        p_new = jnp.exp(s_new - m)
        p = jnp.exp(logits - m)
        l = p_new + jnp.sum(p, axis=1, keepdims=True)
        vt = jnp.concatenate([v_refs[hh * slabs_per_head + i][...] for i in range(slabs_per_head)], axis=1)
        pv = _dot_nt(jnp.broadcast_to(p, (SUBLANES, n_keys)).astype(BF16), vt.astype(BF16))[:1]
        outs.append((p_new * vnew_ref[:, cols] + pv) / l)
    o_ref[...] = jnp.concatenate(outs, axis=1)


def _moba_sample(pt_flat, sel_flat, slopes, cache_kt, cache_vt, q3, knew3, vnew3, *, dec_batch, n_pages):
    def slab_spec(slab):
        hh, rem = divmod(slab, MOBA_TOPK * PAGES_PER_BLOCK)
        r, pg = divmod(rem, PAGES_PER_BLOCK)

        def index_map(b, hp, pt, sel):
            blk = sel[(b * A_HEADS + 2 * hp + hh) * MOBA_TOPK + r]
            return (pt[(blk * PAGES_PER_BLOCK + pg) * dec_batch + b], 2 * hp + hh, 0, 0)

        return pl.BlockSpec((None, None, A_HEAD_DIM, PAGE_SIZE), index_map)

    vec_spec = pl.BlockSpec((None, 1, LANES), lambda b, hp, pt, sel: (b, 0, hp))
    slabs = [slab_spec(i) for i in range(SLABS)]
    grid_spec = pltpu.PrefetchScalarGridSpec(
        num_scalar_prefetch=2,
        grid=(dec_batch, HEAD_PAIRS),
        in_specs=[pl.BlockSpec(memory_space=pltpu.SMEM)] + slabs + slabs + [vec_spec, vec_spec, vec_spec],
        out_specs=vec_spec,
    )
    return pl.pallas_call(
        functools.partial(_moba_sample_kernel, past_len=n_pages * PAGE_SIZE),
        grid_spec=grid_spec,
        out_shape=jax.ShapeDtypeStruct((dec_batch, 1, A_WIDTH), F32),
        compiler_params=pltpu.CompilerParams(dimension_semantics=("arbitrary", "arbitrary"),
                                             vmem_limit_bytes=VMEM_LIMIT),
        name="moba_sample_attn",
    )(pt_flat, sel_flat, slopes, *([cache_kt] * SLABS), *([cache_vt] * SLABS), q3, knew3, vnew3)


MLA_GROUPS = 2
MLA_PAGES_PER_STEP = MLA_GROUPS * PAGES_PER_STEP


def _mla_sample_kernel(pt_ref, *refs, n_steps):
    ckv_pages = refs[:MLA_PAGES_PER_STEP]
    kpe_pages = refs[MLA_PAGES_PER_STEP:2 * MLA_PAGES_PER_STEP]
    (q_ref, ckvn_ref, kpen_ref, wuk_ref, wuv_ref, o_ref, qlat_ref, m_ref, l_ref,
     acc_ref) = refs[2 * MLA_PAGES_PER_STEP:]
    s = pl.program_id(1)
    q8 = q_ref[...]
    q_pe = q8[:, :M_ROPE]

    @pl.when(s == 0)
    def _():
        head_of_lane = lax.broadcasted_iota(jnp.int32, (M_HEADS, M_HEADS * MQ), 1) // MQ
        head_of_row = lax.broadcasted_iota(jnp.int32, (M_HEADS, M_HEADS * MQ), 0)
        q_bd = jnp.where(head_of_lane == head_of_row, jnp.concatenate([q8] * M_HEADS, axis=1), 0.0)
        qlat_ref[...] = _dot_nt(q_bd.astype(BF16), wuk_ref[...])
        m_ref[...] = jnp.full(m_ref.shape, NEG, F32)
        l_ref[...] = jnp.zeros(l_ref.shape, F32)
        acc_ref[...] = jnp.zeros(acc_ref.shape, F32)

    qlat = qlat_ref[...]
    qlat_b = qlat.astype(BF16)
    qpe_b = q_pe.astype(BF16)
    pcs, scores = [], []
    for g in range(MLA_GROUPS):
        pages = range(g * PAGES_PER_STEP, (g + 1) * PAGES_PER_STEP)
        pcs.append(jnp.concatenate([ckv_pages[i][...].astype(BF16) for i in pages], axis=0))
        kpt = jnp.concatenate([kpe_pages[i][...].astype(BF16) for i in pages], axis=1)
        scores.append(_dot_nt(qlat_b, pcs[g]) + _dot(qpe_b, kpt))
    for g in range(MLA_GROUPS):
        m_old = m_ref[g]
        m_new = jnp.maximum(m_old, jnp.max(scores[g], axis=1, keepdims=True))
        alpha = jnp.exp(m_old - m_new)
        p = jnp.exp(scores[g] - m_new)
        m_ref[g] = m_new
        l_ref[g] = alpha * l_ref[g] + jnp.sum(p, axis=1, keepdims=True)
        acc_ref[g] = alpha * acc_ref[g] + _dot(p.astype(BF16), pcs[g])

    @pl.when(s == n_steps - 1)
    def _():
        ckvn = ckvn_ref[...]
        s_new = (jnp.sum(qlat * ckvn, axis=1, keepdims=True)
                 + jnp.sum(q_pe * kpen_ref[...], axis=1, keepdims=True))
        m_fin = s_new
        for g in range(MLA_GROUPS):
            m_fin = jnp.maximum(m_fin, m_ref[g])
        p_new = jnp.exp(s_new - m_fin)
        num = p_new * ckvn
        den = p_new
        for g in range(MLA_GROUPS):
            a = jnp.exp(m_ref[g] - m_fin)
            num = num + a * acc_ref[g]
            den = den + a * l_ref[g]
        o_lat = num / den
        res = _dot(o_lat.astype(BF16), wuv_ref[...])
        head_of_lane = lax.broadcasted_iota(jnp.int32, (M_HEADS, M_WIDTH), 1) // M_VDIM
        head_of_row = lax.broadcasted_iota(jnp.int32, (M_HEADS, M_WIDTH), 0)
        o_ref[...] = jnp.sum(jnp.where(head_of_lane == head_of_row, res, 0.0), axis=0, keepdims=True)


def _mla_sample(pt_flat, cache_ckv, cache_kpet, q8, ckvn3, kpen3, wuk, wuv, *, dec_batch, n_pages):
    assert n_pages % MLA_PAGES_PER_STEP == 0
    n_steps = n_pages // MLA_PAGES_PER_STEP

    def page_spec(i, shape):
        return pl.BlockSpec((None,) + shape,
                            lambda b, s, pt: (pt[(s * MLA_PAGES_PER_STEP + i) * dec_batch + b], 0, 0))

    per_b = lambda shape: pl.BlockSpec((None,) + shape, lambda b, s, pt: (b, 0, 0))
    grid_spec = pltpu.PrefetchScalarGridSpec(
        num_scalar_prefetch=1,
        grid=(dec_batch, n_steps),
        in_specs=([page_spec(i, (PAGE_SIZE, KV_LORA)) for i in range(MLA_PAGES_PER_STEP)]
                  + [page_spec(i, (M_ROPE, PAGE_SIZE)) for i in range(MLA_PAGES_PER_STEP)]
                  + [per_b((M_HEADS, MQ)), per_b((1, KV_LORA)), per_b((1, M_ROPE)),
                     pl.BlockSpec((KV_LORA, M_HEADS * MQ), lambda b, s, pt: (0, 0)),
                     pl.BlockSpec((KV_LORA, M_WIDTH), lambda b, s, pt: (0, 0))]),
        out_specs=per_b((1, M_WIDTH)),
        scratch_shapes=[pltpu.VMEM((M_HEADS, KV_LORA), F32), pltpu.VMEM((MLA_GROUPS, M_HEADS, 1), F32),
                        pltpu.VMEM((MLA_GROUPS, M_HEADS, 1), F32), pltpu.VMEM((MLA_GROUPS, M_HEADS, KV_LORA), F32)],
    )
    return pl.pallas_call(
        functools.partial(_mla_sample_kernel, n_steps=n_steps),
        grid_spec=grid_spec,
        out_shape=jax.ShapeDtypeStruct((dec_batch, 1, M_WIDTH), F32),
        compiler_params=pltpu.CompilerParams(dimension_semantics=("arbitrary", "arbitrary"),
                                             vmem_limit_bytes=VMEM_LIMIT),
        name="mla_sample",
    )(pt_flat, *([cache_ckv] * MLA_PAGES_PER_STEP), *([cache_kpet] * MLA_PAGES_PER_STEP), q8, ckvn3, kpen3, wuk,
      wuv)


def _rot_half_cols(w):
    half = M_ROPE // 2
    return jnp.concatenate([-w[..., half:], w[..., :half]], axis=-1)


def _prep_weights(w_in, w_q_up, w_kv_up):
    kr = w_in[:, C_KR:C_KR + M_ROPE]
    pad = jnp.zeros((D_MODEL, LANES - M_ROPE), F32)
    w1 = jnp.concatenate([w_in[:, :C_KR], kr, pad, _rot_half_cols(kr), pad], axis=1).astype(BF16)
    wqa = w_in[:, C_QA:C_QA + A_WIDTH]
    wq3 = w_q_up.reshape(Q_LORA, M_HEADS, M_NOPE + M_ROPE)
    nope, rope = wq3[..., :M_NOPE], wq3[..., M_NOPE:]
    z = lambda n: jnp.zeros((Q_LORA, M_HEADS, n), F32)
    wq_a = jnp.concatenate([rope, nope, z(MQ - M_NOPE - M_ROPE)], axis=-1).reshape(Q_LORA, M_HEADS * MQ)
    wq_b = jnp.concatenate([_rot_half_cols(rope), z(MQ - M_ROPE)], axis=-1).reshape(Q_LORA, M_HEADS * MQ)
    wq = jnp.concatenate([wq_a, wq_b], axis=1).astype(BF16)
    wkv3 = w_kv_up.reshape(KV_LORA, M_HEADS, M_NOPE + M_VDIM)
    w_uk, w_uv = wkv3[..., :M_NOPE], wkv3[..., M_NOPE:]
    zk = lambda n: jnp.zeros((KV_LORA, M_HEADS, n), F32)
    wuk = jnp.concatenate([zk(M_ROPE), w_uk, zk(MQ - M_ROPE - M_NOPE)], axis=-1).reshape(KV_LORA, M_HEADS * MQ)
    wuv = w_uv.reshape(KV_LORA, M_WIDTH)
    return w1, wqa, wq, wuk.astype(BF16), wuv.astype(BF16), wuv.T.astype(BF16)


def _rope_tables(pos):
    half = M_ROPE // 2
    inv = ROPE_THETA ** (-jnp.arange(half, dtype=F32) / half)
    ang = pos.astype(F32)[:, None] * inv[None, :]
    n = pos.shape[0]
    cos = jnp.concatenate([jnp.cos(ang), jnp.cos(ang), jnp.ones((n, M_NOPE), F32),
                           jnp.zeros((n, MQ - M_ROPE - M_NOPE), F32)], axis=1)
    sin = jnp.concatenate([jnp.sin(ang), jnp.sin(ang), jnp.zeros((n, MQ - M_ROPE), F32)], axis=1)
    return cos, sin


def kernel(x_prompt, x_sample, cache_moba_k, cache_moba_v, cache_mla_ckv, cache_mla_kpe, page_table, g_attn, w_in,
           g_q, w_q_up, g_kv, w_kv_up, g_out_moba, g_out_mla, w_o, g_ffn, w_gate_up, w_down, g_final):
    batch, seq, _ = x_prompt.shape
    dec_batch, dec_seq, _ = x_sample.shape
    n_pool = cache_moba_k.shape[1]
    n_pages = page_table.shape[1]
    past_len = n_pages * PAGE_SIZE
    assert w_in.shape[0] == 1 and dec_seq == 1 and seq % MOBA_BLOCK == 0 and past_len % MOBA_BLOCK == 0

    w1, wqa, wq, wuk, wuv, wuvt = _prep_weights(w_in[0], w_q_up[0], w_kv_up[0])
    wo = w_o[0].astype(BF16)
    wgu = w_gate_up[0].astype(BF16)
    wd = w_down[0].astype(BF16)
    row2 = lambda a: a.reshape(1, -1)
    slopes = 2.0 ** (-8.0 * jnp.arange(1, A_HEADS + 1, dtype=F32) / A_HEADS)

    cos_p, sin_p = _rope_tables(jnp.arange(seq))
    xp = x_prompt.reshape(batch * seq, D_MODEL)
    (kat, vat, ckv, kpet, qaf, qm, qab, kab, vatb, km, vmtb, ksum) = _project(
        xp, cos_p, sin_p, row2(g_attn[0]), w1, row2(g_q[0]), wq, row2(g_kv[0]), (wuk, wuvt),
        prompt=True, tm=MOBA_BLOCK, seq=seq)
    nblk = seq // MOBA_BLOCK
    oa = _moba_prompt(slopes, qaf, qab, ksum.reshape(batch, nblk, A_WIDTH), kab, vatb, batch=batch, seq=seq)
    om = _mla_prompt(qm, km, vmtb, batch=batch, seq=seq)
    finish_w = (row2(g_out_moba[0]), row2(g_out_mla[0]), wo, row2(g_ffn[0]), wgu, wd, row2(g_final))
    y_prompt = _finish(xp, oa, om, *finish_w, tm=256, name="finish_prompt").reshape(batch, seq, D_MODEL)

    cos_s, sin_s = _rope_tables(jnp.full((dec_batch,), past_len, jnp.int32))
    xs = x_sample.reshape(dec_batch, D_MODEL)
    (kat_s, vat_s, ckv_s, kpet_s, qa_s, qm_s, ka_s, va_s, kpe_s, qat_s) = _project(
        xs, cos_s, sin_s, row2(g_attn[0]), w1, row2(g_q[0]), wq, row2(g_kv[0]), (wqa,),
        prompt=False, tm=dec_batch, seq=dec_batch)
    pt_flat = page_table.T.reshape(-1)
    ckt = jnp.transpose(cache_moba_k[0], (0, 2, 3, 1))
    cvt = jnp.transpose(cache_moba_v[0], (0, 2, 3, 1))
    kpet_cache = jnp.transpose(cache_mla_kpe[0], (0, 2, 1))
    q3 = qa_s.reshape(dec_batch, 1, A_WIDTH)
    knew3 = ka_s.reshape(dec_batch, 1, A_WIDTH)
    vnew3 = va_s.reshape(dec_batch, 1, A_WIDTH)
    sel = _moba_gate(pt_flat, ckt.reshape(n_pool, A_WIDTH, PAGE_SIZE), qat_s, q3, knew3,
                     dec_batch=dec_batch, n_pages=n_pages)
    oa_s = _moba_sample(pt_flat, sel[:, :A_HEADS, :MOBA_TOPK].reshape(-1), slopes, ckt, cvt, q3, knew3, vnew3,
                        dec_batch=dec_batch, n_pages=n_pages)
    om_s = _mla_sample(pt_flat, cache_mla_ckv[0], kpet_cache, qm_s.reshape(dec_batch, M_HEADS, MQ),
                       ckv_s.reshape(dec_batch, 1, KV_LORA), kpe_s.reshape(dec_batch, 1, M_ROPE), wuk, wuv,
                       dec_batch=dec_batch, n_pages=n_pages)
    y_sample = _finish(xs, oa_s.reshape(dec_batch, A_WIDTH), om_s.reshape(dec_batch, M_WIDTH), *finish_w,
                       tm=dec_batch, name="finish_sample").reshape(dec_batch, 1, D_MODEL)

    def heads_last(t_minor, n_tok_batch, n_tok):
        return jnp.transpose(t_minor.reshape(1, n_tok_batch, A_HEADS, A_HEAD_DIM, n_tok), (0, 1, 4, 2, 3))

    k_sample = jnp.transpose(kat_s.reshape(1, 1, A_HEADS, A_HEAD_DIM, dec_batch), (0, 4, 1, 2, 3))
    v_sample = jnp.transpose(vat_s.reshape(1, 1, A_HEADS, A_HEAD_DIM, dec_batch), (0, 4, 1, 2, 3))
    return (y_prompt, y_sample,
            heads_last(kat, batch, seq), heads_last(vat, batch, seq),
            ckv.reshape(1, batch, seq, KV_LORA), jnp.transpose(kpet.reshape(1, batch, M_ROPE, seq), (0, 1, 3, 2)),
            k_sample, v_sample,
            ckv_s.reshape(1, dec_batch, 1, KV_LORA),
            jnp.transpose(kpet_s.reshape(1, 1, M_ROPE, dec_batch), (0, 3, 1, 2)))
```

```python
import functools

import jax
import jax.numpy as jnp
from jax import lax
from jax.experimental import pallas as pl
from jax.experimental.pallas import tpu as pltpu

F32 = jnp.float32
BF16 = jnp.bfloat16

D_MODEL = 1024
A_HEADS = 8
A_HEAD_DIM = 64
A_WIDTH = A_HEADS * A_HEAD_DIM
MOBA_BLOCK = 256
MOBA_TOPK = 3
M_HEADS = 8
M_NOPE = 64
M_ROPE = 32
M_VDIM = 64
M_WIDTH = M_HEADS * M_VDIM
Q_LORA = 256
KV_LORA = 128
ROPE_THETA = 10000.0
D_FF = 2816
PAGE_SIZE = 128
EPS = 1e-6

LANES = 128
SUBLANES = 8
HEAD_PAIRS = A_HEADS // 2
MQ = 128
NEG = -1e30
LOG2E = 1.4426950408889634
VMEM_LIMIT = 56 * 1024 * 1024

C_QA, C_KA, C_VA = 0, A_WIDTH, 2 * A_WIDTH
C_QD = 3 * A_WIDTH
C_KVD = C_QD + Q_LORA
C_KR = C_KVD + KV_LORA
C_KRR = C_KR + LANES
N_W1 = C_KRR + LANES


def _rms(x, g):
    return x * lax.rsqrt(jnp.mean(x * x, axis=-1, keepdims=True) + EPS) * g


def _dot(a, b, precision=None):
    return jnp.dot(a, b, preferred_element_type=F32, precision=precision)


def _dot_nt(a, b, precision=None):
    return lax.dot_general(a, b, (((1,), (1,)), ((), ())), preferred_element_type=F32, precision=precision)


def _proj_kernel(x_ref, cos_ref, sin_ref, ga_ref, w1_ref, gq_ref, wq_ref, gkv_ref, *rest, prompt, tm, qk_scale):
    if prompt:
        (wuk_ref, wuvt_ref, kat_o, vat_o, ckv_o, kpet_o, qaf_o, qm_o, qab_o, kab_o, vatb_o, km_o, vmtb_o,
         ksum_o) = rest
    else:
        wqa_ref, kat_o, vat_o, ckv_o, kpet_o, qaf_o, qm_o, ka_o, va_o, kpe_o, qat_o = rest
    xn = _rms(x_ref[...], ga_ref[...])
    xb = xn.astype(BF16)
    z = _dot(xb, w1_ref[...])
    ka = z[:, C_KA:C_KA + A_WIDTH]
    va = z[:, C_VA:C_VA + A_WIDTH]
    if prompt:
        qa = z[:, C_QA:C_QA + A_WIDTH]
    else:
        qa = _dot(xn, wqa_ref[...], precision=lax.Precision.HIGHEST)
    cos = cos_ref[...]
    sin = sin_ref[...]
    kpe = z[:, C_KR:C_KR + LANES] * cos + z[:, C_KRR:C_KRR + LANES] * sin
    ckv = _rms(z[:, C_KVD:C_KVD + KV_LORA], gkv_ref[...])
    qn = _rms(z[:, C_QD:C_QD + Q_LORA], gq_ref[...]).astype(BF16)
    qab = _dot(qn, wq_ref[...])
    kat = ka.T
    vat = va.T
    kat_o[...] = kat
    vat_o[...] = vat
    ckv_o[...] = ckv
    kpet_o[...] = kpe.T[:M_ROPE]
    qaf_o[...] = qa
    for h in range(M_HEADS):
        sl = slice(h * MQ, (h + 1) * MQ)
        q_h = (qab[:, sl] * cos + qab[:, M_HEADS * MQ + h * MQ:M_HEADS * MQ + (h + 1) * MQ] * sin) * qk_scale
        qm_o[:, sl] = q_h.astype(qm_o.dtype)
    if prompt:
        ckv_b = ckv.astype(BF16)
        qab_o[...] = (qa * (A_HEAD_DIM ** -0.5 * LOG2E)).astype(BF16)
        kab_o[...] = ka.astype(BF16)
        kn = _dot(ckv_b, wuk_ref[...])
        for h in range(M_HEADS):
            sl = slice(h * MQ, (h + 1) * MQ)
            km_o[:, sl] = (kn[:, sl] + kpe).astype(BF16)
        vatb = vat.astype(BF16)
        vmtb = _dot_nt(wuvt_ref[...], ckv_b).astype(BF16)
        for j in range(tm // MOBA_BLOCK):
            rows = slice(j * MOBA_BLOCK, (j + 1) * MOBA_BLOCK)
            vatb_o[j] = vatb[:, rows]
            vmtb_o[j] = vmtb[:, rows]
            ksum_o[j] = jnp.sum(ka[rows], axis=0, keepdims=True)
    else:
        ka_o[...] = ka
        va_o[...] = va
        kpe_o[...] = kpe[:, :M_ROPE]
        qat_o[...] = qa.T


def _const_spec(shape):
    return pl.BlockSpec(shape, lambda *_: (0,) * len(shape))


def _project(x2d, cos_tab, sin_tab, g_attn, w1, g_q, wq, g_kv, extra_weights, *, prompt, tm, seq):
    t = x2d.shape[0]
    assert t % tm == 0 and seq % tm == 0
    n_pos_tiles = seq // tm
    batch = t // seq
    row = lambda i: (i, 0)
    tok_minor = lambda i: (i // n_pos_tiles, 0, i % n_pos_tiles)
    in_specs = [
        pl.BlockSpec((tm, D_MODEL), row),
        pl.BlockSpec((tm, LANES), lambda i: (i % n_pos_tiles, 0)),
        pl.BlockSpec((tm, LANES), lambda i: (i % n_pos_tiles, 0)),
        _const_spec((1, D_MODEL)),
        _const_spec((D_MODEL, N_W1)),
        _const_spec((1, Q_LORA)),
        _const_spec((Q_LORA, 2 * M_HEADS * MQ)),
        _const_spec((1, KV_LORA)),
    ]
    args = [x2d, cos_tab, sin_tab, g_attn, w1, g_q, wq, g_kv]
    f32_out = lambda n: jax.ShapeDtypeStruct((t, n), F32)
    tm_out = lambda n: jax.ShapeDtypeStruct((batch, n, seq), F32)
    out_shape = [tm_out(A_WIDTH), tm_out(A_WIDTH), f32_out(KV_LORA), tm_out(M_ROPE), f32_out(A_WIDTH)]
    out_specs = [pl.BlockSpec((None, A_WIDTH, tm), tok_minor), pl.BlockSpec((None, A_WIDTH, tm), tok_minor),
                 pl.BlockSpec((tm, KV_LORA), row), pl.BlockSpec((None, M_ROPE, tm), tok_minor),
                 pl.BlockSpec((tm, A_WIDTH), row)]
    if prompt:
        assert tm % MOBA_BLOCK == 0
        wuk, wuvt = extra_weights
        nb_tile = tm // MOBA_BLOCK
        nblk = seq // MOBA_BLOCK
        in_specs += [_const_spec((KV_LORA, M_HEADS * MQ)), _const_spec((M_WIDTH, KV_LORA))]
        args += [wuk, wuvt]
        bf_out = lambda n: jax.ShapeDtypeStruct((t, n), BF16)
        vt_shape = jax.ShapeDtypeStruct((batch, nblk, A_WIDTH, MOBA_BLOCK), BF16)
        vt_spec = pl.BlockSpec((None, nb_tile, A_WIDTH, MOBA_BLOCK),
                               lambda i: (i // n_pos_tiles, i % n_pos_tiles, 0, 0))
        out_shape += [bf_out(M_HEADS * MQ), bf_out(A_WIDTH), bf_out(A_WIDTH), vt_shape, bf_out(M_HEADS * MQ),
                      vt_shape, jax.ShapeDtypeStruct((t // MOBA_BLOCK, 1, A_WIDTH), F32)]
        out_specs += [pl.BlockSpec((tm, M_HEADS * MQ), row), pl.BlockSpec((tm, A_WIDTH), row),
                      pl.BlockSpec((tm, A_WIDTH), row), vt_spec, pl.BlockSpec((tm, M_HEADS * MQ), row), vt_spec,
                      pl.BlockSpec((nb_tile, 1, A_WIDTH), lambda i: (i, 0, 0))]
    else:
        assert batch == 1
        (wqa,) = extra_weights
        in_specs += [_const_spec((D_MODEL, A_WIDTH))]
        args += [wqa]
        out_shape += [f32_out(M_HEADS * MQ), f32_out(A_WIDTH), f32_out(A_WIDTH), f32_out(M_ROPE),
                      jax.ShapeDtypeStruct((A_WIDTH, t), F32)]
        out_specs += [pl.BlockSpec((tm, M_HEADS * MQ), row), pl.BlockSpec((tm, A_WIDTH), row),
                      pl.BlockSpec((tm, A_WIDTH), row), pl.BlockSpec((tm, M_ROPE), row),
                      pl.BlockSpec((A_WIDTH, tm), lambda i: (0, i))]
    return pl.pallas_call(
        functools.partial(_proj_kernel, prompt=prompt, tm=tm,
                          qk_scale=(M_NOPE + M_ROPE) ** -0.5 * (LOG2E if prompt else 1.0)),
        grid=(t // tm,),
        in_specs=in_specs,
        out_specs=out_specs,
        out_shape=out_shape,
        compiler_params=pltpu.CompilerParams(dimension_semantics=("arbitrary",), vmem_limit_bytes=VMEM_LIMIT),
        name="proj_prompt" if prompt else "proj_sample",
    )(*args)


ATTN_CHUNK = 2
CHUNKS_PER_STEP = 4


ONES_ROWS = 16


def _run_chains(carry, chains, logits_fn, values_fn):
    blk = MOBA_BLOCK
    carry = list(carry)
    ones = jnp.ones((ONES_ROWS, blk), BF16)
    scores = [logits_fn(*chain) for chain in chains]
    for (c, hh, _), logits in zip(chains, scores):
        m, acc = carry[hh]
        m_new = jnp.maximum(m, jnp.max(logits, axis=0, keepdims=True))
        pb = jnp.exp2(logits - m_new).astype(BF16)
        vt_parts = values_fn(c, hh)
        pv = None
        for t, vt in enumerate(vt_parts):
            part = _dot(jnp.concatenate([vt, ones], axis=0), pb[t * blk:(t + 1) * blk])
            pv = part if pv is None else pv + part
        carry[hh] = (m_new, jnp.exp2(m - m_new) * acc + pv)
    return tuple(carry)


def _attend_causal(qi, chunk, dv, logits_fn, values_fn):
    blk = MOBA_BLOCK
    heads = range(2)
    c_diag = qi // chunk
    n_past_steps = c_diag // CHUNKS_PER_STEP

    def step_chains(i, n_chunks, mask_last):
        return [(i * CHUNKS_PER_STEP + u, hh, mask_last and u == n_chunks - 1)
                for u in range(n_chunks) for hh in heads]

    init = tuple((jnp.full((1, blk), NEG, F32), jnp.zeros((dv + ONES_ROWS, blk), F32)) for _ in heads)
    last_step = [functools.partial(_run_chains, chains=step_chains(n_past_steps, n, True),
                                   logits_fn=logits_fn, values_fn=values_fn)
                 for n in range(1, CHUNKS_PER_STEP + 1)]
    carry = lax.switch(c_diag % CHUNKS_PER_STEP, last_step, init)
    carry = lax.fori_loop(
        0, n_past_steps,
        lambda i, cr: _run_chains(cr, step_chains(i, CHUNKS_PER_STEP, False), logits_fn, values_fn), carry)
    return jnp.concatenate([acc[:dv] / acc[dv:dv + 1] for _, acc in carry], axis=0)


def _chunk_key_minus_query(chunk):
    shape = (chunk * MOBA_BLOCK, MOBA_BLOCK)
    return lax.broadcasted_iota(jnp.int32, shape, 0) - lax.broadcasted_iota(jnp.int32, shape, 1)


def _key_minus_query(blk):
    return (lax.broadcasted_iota(jnp.int32, (blk, blk), 0) - lax.broadcasted_iota(jnp.int32, (blk, blk), 1))


def _top_blocks(gate, blk_id, n_cand):
    picks = []
    g = gate
    for _ in range(MOBA_TOPK):
        top = jnp.max(g, axis=0, keepdims=True)
        first = jnp.min(jnp.where(g == top, blk_id, n_cand), axis=0, keepdims=True)
        picks.append(first)
        g = jnp.where(blk_id == first, -jnp.inf, g)
    return picks


AUG_SLOPE_ROWS = SUBLANES


def _bf16_split3(x):
    hi = x.astype(BF16).astype(F32)
    mid = (x - hi).astype(BF16).astype(F32)
    lo = (x - hi - mid).astype(BF16).astype(F32)
    return hi, mid, lo


def _moba_key_bias_columns(seq):
    nblk = seq // MOBA_BLOCK
    assert AUG_SLOPE_ROWS + 3 * nblk <= LANES
    pos = jnp.arange(seq)
    col = jnp.arange(LANES)[None, :]
    in_blk = (col >= AUG_SLOPE_ROWS) & (col < AUG_SLOPE_ROWS + 3 * nblk)
    blk_of_col = (col - AUG_SLOPE_ROWS) % nblk
    aug = jnp.where(col < 3, (pos % MOBA_BLOCK)[:, None], 0)
    aug = jnp.where(in_blk & (blk_of_col == (pos // MOBA_BLOCK)[:, None]), 1, aug)
    return aug.astype(BF16)


def _moba_query_bias_rows(attended, slope2, qi, nblk):
    blk = MOBA_BLOCK
    blk_id = lax.broadcasted_iota(jnp.int32, (nblk, blk), 0)
    q_off = lax.broadcasted_iota(jnp.int32, (nblk, blk), 1).astype(F32)
    per_block = jnp.where(attended, 0.0, NEG) + slope2 * ((blk_id - qi) * blk).astype(F32) - slope2 * q_off
    row = lax.broadcasted_iota(jnp.int32, (AUG_SLOPE_ROWS, blk), 0)
    s_hi, s_mid, s_lo = _bf16_split3(jnp.full((AUG_SLOPE_ROWS, blk), slope2, F32))
    slope_rows = jnp.where(row == 0, s_hi, jnp.where(row == 1, s_mid, jnp.where(row == 2, s_lo, 0.0)))
    pad = jnp.zeros((LANES - AUG_SLOPE_ROWS - 3 * nblk, blk), F32)
    return jnp.concatenate([slope_rows, *_bf16_split3(per_block), pad], axis=0)


def _moba_prompt_kernel(slopes_ref, qf_ref, qb_ref, ksum_ref, k_ref, kaug_ref, vt_ref, o_ref, *, nblk, chunk):
    blk = MOBA_BLOCK
    ck = chunk * blk
    hp = pl.program_id(1)
    qi = pl.program_id(2)
    lane = lax.broadcasted_iota(jnp.int32, (1, LANES), 1)
    kmean = ksum_ref[...] * (1.0 / blk)
    blk_id = lax.broadcasted_iota(jnp.int32, (nblk, blk), 0)
    past = blk_id < qi
    heads = range(2)
    q = []
    for hh in heads:
        head_lanes = (lane >= A_HEAD_DIM) if hh else (lane < A_HEAD_DIM)
        q_h = jnp.where(head_lanes, qb_ref[...], jnp.zeros((), BF16))
        qf_h = jnp.where(head_lanes, qf_ref[...], 0.0)
        gate = _dot_nt(kmean, qf_h, precision=lax.Precision.HIGHEST)
        attended = blk_id == qi
        for first in _top_blocks(jnp.where(past, gate, -jnp.inf), blk_id, nblk):
            attended = jnp.logical_or(attended, jnp.logical_and(blk_id == first, past))
        bias_rows = _moba_query_bias_rows(attended, slopes_ref[2 * hp + hh] * LOG2E, qi, nblk)
        q.append(jnp.concatenate([q_h, bias_rows.T.astype(BF16)], axis=1))

    rel = _chunk_key_minus_query(chunk)

    def logits(c, hh, masked):
        rows = pl.ds(pl.multiple_of(c * ck, ck), ck)
        keys = jnp.concatenate([k_ref[rows, :], kaug_ref[rows, :]], axis=1)
        s = _dot_nt(keys, q[hh])
        return jnp.where(rel <= (qi - c * chunk) * blk, s, NEG) if masked else s

    def values(c, hh):
        return [vt_ref[c * chunk + t, hh * A_HEAD_DIM:(hh + 1) * A_HEAD_DIM, :] for t in range(chunk)]

    o_ref[...] = _attend_causal(qi, chunk, A_HEAD_DIM, logits, values).T


def _moba_prompt(slopes, qaf, qab, ksum, kab, vatb, *, batch, seq):
    nblk = seq // MOBA_BLOCK
    blk = MOBA_BLOCK
    assert nblk % (ATTN_CHUNK * CHUNKS_PER_STEP) == 0
    q_spec = pl.BlockSpec((blk, LANES), lambda b, hp, qi: (b * nblk + qi, hp))
    return pl.pallas_call(
        functools.partial(_moba_prompt_kernel, nblk=nblk, chunk=ATTN_CHUNK),
        grid=(batch, HEAD_PAIRS, nblk),
        in_specs=[
            pl.BlockSpec(memory_space=pltpu.SMEM),
            q_spec,
            q_spec,
            pl.BlockSpec((None, nblk, LANES), lambda b, hp, qi: (b, 0, hp)),
            pl.BlockSpec((seq, LANES), lambda b, hp, qi: (b, hp)),
            pl.BlockSpec((seq, LANES), lambda b, hp, qi: (0, 0)),
            pl.BlockSpec((None, nblk, LANES, blk), lambda b, hp, qi: (b, 0, hp, 0)),
        ],
        out_specs=q_spec,
        out_shape=jax.ShapeDtypeStruct((batch * seq, A_WIDTH), F32),
        compiler_params=pltpu.CompilerParams(dimension_semantics=("arbitrary",) * 3, vmem_limit_bytes=VMEM_LIMIT),
        name="moba_prompt",
    )(slopes, qaf, qab, ksum, kab, _moba_key_bias_columns(seq), vatb)


def _mla_prompt_kernel(q_ref, k_ref, vt_ref, o_ref, *, chunk):
    blk = MOBA_BLOCK
    ck = chunk * blk
    qi = pl.program_id(2)
    heads = range(2)
    q = [q_ref[:, hh * MQ:(hh + 1) * MQ] for hh in heads]

    rel = _chunk_key_minus_query(chunk)

    def logits(c, hh, masked):
        rows = pl.ds(pl.multiple_of(c * ck, ck), ck)
        s = _dot_nt(k_ref[rows, hh * MQ:(hh + 1) * MQ], q[hh])
        return jnp.where(rel <= (qi - c * chunk) * blk, s, NEG) if masked else s

    def values(c, hh):
        return [vt_ref[c * chunk + t, hh * M_VDIM:(hh + 1) * M_VDIM, :] for t in range(chunk)]

    o_ref[...] = _attend_causal(qi, chunk, M_VDIM, logits, values).T


def _mla_prompt(qm, km, vmtb, *, batch, seq):
    nblk = seq // MOBA_BLOCK
    blk = MOBA_BLOCK
    assert nblk % (ATTN_CHUNK * CHUNKS_PER_STEP) == 0
    return pl.pallas_call(
        functools.partial(_mla_prompt_kernel, chunk=ATTN_CHUNK),
        grid=(batch, HEAD_PAIRS, nblk),
        in_specs=[
            pl.BlockSpec((blk, 2 * MQ), lambda b, hp, qi: (b * nblk + qi, hp)),
            pl.BlockSpec((seq, 2 * MQ), lambda b, hp, qi: (b, hp)),
            pl.BlockSpec((None, nblk, LANES, blk), lambda b, hp, qi: (b, 0, hp, 0)),
        ],
        out_specs=pl.BlockSpec((blk, LANES), lambda b, hp, qi: (b * nblk + qi, hp)),
        out_shape=jax.ShapeDtypeStruct((batch * seq, M_WIDTH), F32),
        compiler_params=pltpu.CompilerParams(dimension_semantics=("arbitrary",) * 3, vmem_limit_bytes=VMEM_LIMIT),
        name="mla_prompt",
    )(qm, km, vmtb)


def _finish_kernel(x_ref, oa_ref, om_ref, goa_ref, gom_ref, wo_ref, gf_ref, wgu_ref, wd_ref, gfin_ref, y_ref):
    oa = _rms(oa_ref[...], goa_ref[...])
    om = _rms(om_ref[...], gom_ref[...])
    mix = jnp.concatenate([oa, om], axis=-1).astype(BF16)
    h = x_ref[...] + _dot(mix, wo_ref[...])
    hn = _rms(h, gf_ref[...]).astype(BF16)
    gu = _dot(hn, wgu_ref[...])
    g = gu[:, :D_FF]
    u = gu[:, D_FF:]
    act = (g / (1.0 + jnp.exp(-g)) * u).astype(BF16)
    h = h + _dot(act, wd_ref[...])
    y_ref[...] = _rms(h, gfin_ref[...])


def _finish(x2d, oa, om, g_oa, g_om, wo, g_ffn, wgu, wd, g_final, *, tm, name):
    t = x2d.shape[0]
    assert t % tm == 0
    row = lambda i: (i, 0)
    single = pl.Buffered(1)
    wspec = lambda shape: pl.BlockSpec(shape, lambda i: (0, 0), pipeline_mode=single)
    return pl.pallas_call(
        _finish_kernel,
        grid=(t // tm,),
        in_specs=[
            pl.BlockSpec((tm, D_MODEL), row),
            pl.BlockSpec((tm, A_WIDTH), row),
            pl.BlockSpec((tm, M_WIDTH), row),
            _const_spec((1, A_WIDTH)),
            _const_spec((1, M_WIDTH)),
            wspec((A_WIDTH + M_WIDTH, D_MODEL)),
            _const_spec((1, D_MODEL)),
            wspec((D_MODEL, 2 * D_FF)),
            wspec((D_FF, D_MODEL)),
            _const_spec((1, D_MODEL)),
        ],
        out_specs=pl.BlockSpec((tm, D_MODEL), row),
        out_shape=jax.ShapeDtypeStruct((t, D_MODEL), F32),
        compiler_params=pltpu.CompilerParams(dimension_semantics=("arbitrary",), vmem_limit_bytes=VMEM_LIMIT),
        name=name,
    )(x2d, oa, om, g_oa, g_om, wo, g_ffn, wgu, wd, g_final)


PAGES_PER_STEP = 16
PAGES_PER_BLOCK = MOBA_BLOCK // PAGE_SIZE
BLOCKS_PER_STEP = PAGES_PER_STEP // PAGES_PER_BLOCK


def _moba_gate_kernel(pt_ref, *refs, n_steps, nblk):
    pages = refs[:PAGES_PER_STEP]
    qat_ref, q_ref, knew_ref, sel_ref, score_ref, qb_ref, part_ref = refs[PAGES_PER_STEP:]
    b = pl.program_id(0)
    s = pl.program_id(1)
    lane = lax.broadcasted_iota(jnp.int32, (1, LANES), 1)

    @pl.when(s == 0)
    def _():
        seq_id = lax.broadcasted_iota(jnp.int32, (1, qat_ref.shape[1]), 1)
        col = jnp.sum(jnp.where(seq_id == b, qat_ref[...], 0.0), axis=1, keepdims=True)
        qb_ref[...] = jnp.broadcast_to(col, qb_ref.shape)

    qb = qb_ref[...]
    for i in range(BLOCKS_PER_STEP):
        tiles = []
        for pg in range(PAGES_PER_BLOCK):
            prod = pages[PAGES_PER_BLOCK * i + pg][...] * qb
            tiles.append(jnp.concatenate(
                [jnp.sum(prod[h * A_HEAD_DIM:(h + 1) * A_HEAD_DIM], axis=0, keepdims=True)
                 for h in range(A_HEADS)], axis=0))
            score_ref[PAGES_PER_BLOCK * i + pg] = tiles[pg]
        part_ref[pl.ds(pl.multiple_of((s * BLOCKS_PER_STEP + i) * A_HEADS, A_HEADS), A_HEADS), :] = (
            functools.reduce(jnp.add, tiles))

    @pl.when(s == n_steps - 1)
    def _():
        gate = jnp.sum(part_ref[...], axis=1, keepdims=True) * (1.0 / MOBA_BLOCK)
        prod_new = knew_ref[...] * q_ref[...]
        lane_head = lax.broadcasted_iota(jnp.int32, (1, A_WIDTH), 1) // A_HEAD_DIM
        head_row = lax.broadcasted_iota(jnp.int32, (A_HEADS, 1), 0)
        g_new = jnp.zeros((A_HEADS, 1), F32)
        for h in range(A_HEADS):
            g_h = jnp.sum(jnp.where(lane_head == h, prod_new, 0.0), axis=1, keepdims=True)
            g_new = jnp.where(head_row == h, g_h * (1.0 / MOBA_BLOCK), g_new)
        n_cand = nblk + 1
        cand = [gate[j * A_HEADS:(j + 1) * A_HEADS] for j in range(nblk)] + [g_new]
        cand = [g if j < nblk else jnp.full_like(g, -jnp.inf) for j, g in enumerate(cand)]
        out = jnp.zeros((A_HEADS, LANES), jnp.int32)
        for r in range(MOBA_TOPK):
            top = functools.reduce(jnp.maximum, cand)
            first = functools.reduce(jnp.minimum, [jnp.where(g == top, j, n_cand) for j, g in enumerate(cand)])
            out = jnp.where(lane == r, first, out)
            cand = [jnp.where(first == j, -jnp.inf, g) for j, g in enumerate(cand)]
        sel_ref[...] = out


def _moba_gate(pt_flat, cache_kt, qat, q3, knew3, *, dec_batch, n_pages):
    nblk = n_pages // PAGES_PER_BLOCK
    assert n_pages % PAGES_PER_STEP == 0 and nblk >= MOBA_TOPK and dec_batch <= LANES
    n_steps = n_pages // PAGES_PER_STEP

    def page_spec(i):
        return pl.BlockSpec((None, A_WIDTH, PAGE_SIZE),
                            lambda b, s, pt: (pt[(s * PAGES_PER_STEP + i) * dec_batch + b], 0, 0))

    vec_spec = pl.BlockSpec((None, 1, A_WIDTH), lambda b, s, pt: (b, 0, 0))
    grid_spec = pltpu.PrefetchScalarGridSpec(
        num_scalar_prefetch=1,
        grid=(dec_batch, n_steps),
        in_specs=([page_spec(i) for i in range(PAGES_PER_STEP)]
                  + [pl.BlockSpec((A_WIDTH, dec_batch), lambda b, s, pt: (0, 0)), vec_spec, vec_spec]),
        out_specs=[pl.BlockSpec((None, SUBLANES, LANES), lambda b, s, pt: (b, 0, 0)),
                   pl.BlockSpec((None, PAGES_PER_STEP, A_HEADS, PAGE_SIZE), lambda b, s, pt: (b, s, 0, 0))],
        scratch_shapes=[pltpu.VMEM((A_WIDTH, LANES), F32), pltpu.VMEM((nblk * A_HEADS, LANES), F32)],
    )
    return pl.pallas_call(
        functools.partial(_moba_gate_kernel, n_steps=n_steps, nblk=nblk),
        grid_spec=grid_spec,
        out_shape=[jax.ShapeDtypeStruct((dec_batch, SUBLANES, LANES), jnp.int32),
                   jax.ShapeDtypeStruct((dec_batch, n_pages, A_HEADS, PAGE_SIZE), F32)],
        compiler_params=pltpu.CompilerParams(dimension_semantics=("arbitrary", "arbitrary"),
                                             vmem_limit_bytes=VMEM_LIMIT),
        name="moba_sample_gate",
    )(pt_flat, *([cache_kt] * PAGES_PER_STEP), qat, q3, knew3)


SLABS = 2 * MOBA_TOPK * PAGES_PER_BLOCK


def _moba_sample_kernel(pt_ref, sel_ref, slopes_ref, *refs, past_len):
    v_refs = refs[:SLABS]
    score_ref, q_ref, knew_ref, vnew_ref, o_ref = refs[SLABS:]
    b = pl.program_id(0)
    hp = pl.program_id(1)
    n_keys = MOBA_TOPK * MOBA_BLOCK
    slabs_per_head = MOBA_TOPK * PAGES_PER_BLOCK
    lane = lax.broadcasted_iota(jnp.int32, (1, n_keys), 1)
    scale = A_HEAD_DIM ** -0.5
    outs = []
    for hh in range(2):
        h = 2 * hp + hh
        cols = slice(hh * A_HEAD_DIM, (hh + 1) * A_HEAD_DIM)
        kpos = lane % MOBA_BLOCK
        pieces = []
        for r in range(MOBA_TOPK):
            blk = sel_ref[(b * A_HEADS + h) * MOBA_TOPK + r]
            kpos = kpos + jnp.where(lane // MOBA_BLOCK == r, blk * MOBA_BLOCK, 0)
            pieces += [score_ref[blk * PAGES_PER_BLOCK + pg, pl.ds(h, 1), :] for pg in range(PAGES_PER_BLOCK)]
        logits = jnp.concatenate(pieces, axis=1) * scale - slopes_ref[h] * (past_len - kpos).astype(F32)
        s_new = jnp.sum(knew_ref[:, cols] * q_ref[:, cols], axis=1, keepdims=True) * scale
        m = jnp.maximum(s_new, jnp.max(logits, axis=1, keepdims=True))
        p_new = jnp.exp(s_new - m)
        p = jnp.exp(logits - m)
        l = p_new + jnp.sum(p, axis=1, keepdims=True)
        vt = jnp.concatenate([v_refs[hh * slabs_per_head + i][...] for i in range(slabs_per_head)], axis=1)
        pv = _dot_nt(jnp.broadcast_to(p, (SUBLANES, n_keys)).astype(BF16), vt.astype(BF16))[:1]
        outs.append((p_new * vnew_ref[:, cols] + pv) / l)
    o_ref[...] = jnp.concatenate(outs, axis=1)


def _moba_sample(pt_flat, sel_flat, slopes, cache_vt, scores, q3, knew3, vnew3, *, dec_batch, n_pages):
    def slab_spec(slab):
        hh, rem = divmod(slab, MOBA_TOPK * PAGES_PER_BLOCK)
        r, pg = divmod(rem, PAGES_PER_BLOCK)

        def index_map(b, hp, pt, sel):
            blk = sel[(b * A_HEADS + 2 * hp + hh) * MOBA_TOPK + r]
            return (pt[(blk * PAGES_PER_BLOCK + pg) * dec_batch + b], 2 * hp + hh, 0, 0)

        return pl.BlockSpec((None, None, A_HEAD_DIM, PAGE_SIZE), index_map)

    vec_spec = pl.BlockSpec((None, 1, LANES), lambda b, hp, pt, sel: (b, 0, hp))
    slabs = [slab_spec(i) for i in range(SLABS)]
    grid_spec = pltpu.PrefetchScalarGridSpec(
        num_scalar_prefetch=2,
        grid=(dec_batch, HEAD_PAIRS),
        in_specs=([pl.BlockSpec(memory_space=pltpu.SMEM)] + slabs
                  + [pl.BlockSpec((None, n_pages, A_HEADS, PAGE_SIZE), lambda b, hp, pt, sel: (b, 0, 0, 0)),
                     vec_spec, vec_spec, vec_spec]),
        out_specs=vec_spec,
    )
    return pl.pallas_call(
        functools.partial(_moba_sample_kernel, past_len=n_pages * PAGE_SIZE),
        grid_spec=grid_spec,
        out_shape=jax.ShapeDtypeStruct((dec_batch, 1, A_WIDTH), F32),
        compiler_params=pltpu.CompilerParams(dimension_semantics=("arbitrary", "arbitrary"),
                                             vmem_limit_bytes=VMEM_LIMIT),
        name="moba_sample_attn",
    )(pt_flat, sel_flat, slopes, *([cache_vt] * SLABS), scores, q3, knew3, vnew3)


MLA_GROUPS = 2
MLA_PAGES_PER_STEP = MLA_GROUPS * PAGES_PER_STEP


def _mla_sample_kernel(pt_ref, *refs, n_steps):
    ckv_pages = refs[:MLA_PAGES_PER_STEP]
    kpe_pages = refs[MLA_PAGES_PER_STEP:2 * MLA_PAGES_PER_STEP]
    (q_ref, ckvn_ref, kpen_ref, wuk_ref, wuv_ref, o_ref, qlat_ref, m_ref, l_ref,
     acc_ref) = refs[2 * MLA_PAGES_PER_STEP:]
    s = pl.program_id(1)
    q8 = q_ref[...]
    q_pe = q8[:, :M_ROPE]

    @pl.when(s == 0)
    def _():
        head_of_lane = lax.broadcasted_iota(jnp.int32, (M_HEADS, M_HEADS * MQ), 1) // MQ
        head_of_row = lax.broadcasted_iota(jnp.int32, (M_HEADS, M_HEADS * MQ), 0)
        q_bd = jnp.where(head_of_lane == head_of_row, jnp.concatenate([q8] * M_HEADS, axis=1), 0.0)
        qlat_ref[...] = _dot_nt(q_bd.astype(BF16), wuk_ref[...])
        m_ref[...] = jnp.full(m_ref.shape, NEG, F32)
        l_ref[...] = jnp.zeros(l_ref.shape, F32)
        acc_ref[...] = jnp.zeros(acc_ref.shape, F32)

    qlat = qlat_ref[...]
    qlat_b = qlat.astype(BF16)
    qpe_b = q_pe.astype(BF16)
    pcs, scores = [], []
    for g in range(MLA_GROUPS):
        pages = range(g * PAGES_PER_STEP, (g + 1) * PAGES_PER_STEP)
        pcs.append(jnp.concatenate([ckv_pages[i][...].astype(BF16) for i in pages], axis=0))
        kpt = jnp.concatenate([kpe_pages[i][...].astype(BF16) for i in pages], axis=1)
        scores.append(_dot_nt(qlat_b, pcs[g]) + _dot(qpe_b, kpt))
    for g in range(MLA_GROUPS):
        m_old = m_ref[g]
        m_new = jnp.maximum(m_old, jnp.max(scores[g], axis=1, keepdims=True))
        alpha = jnp.exp(m_old - m_new)
        p = jnp.exp(scores[g] - m_new)
        m_ref[g] = m_new
        l_ref[g] = alpha * l_ref[g] + jnp.sum(p, axis=1, keepdims=True)
        acc_ref[g] = alpha * acc_ref[g] + _dot(p.astype(BF16), pcs[g])

    @pl.when(s == n_steps - 1)
    def _():
        ckvn = ckvn_ref[...]
        s_new = (jnp.sum(qlat * ckvn, axis=1, keepdims=True)
                 + jnp.sum(q_pe * kpen_ref[...], axis=1, keepdims=True))
        m_fin = s_new
        for g in range(MLA_GROUPS):
            m_fin = jnp.maximum(m_fin, m_ref[g])
        p_new = jnp.exp(s_new - m_fin)
        num = p_new * ckvn
        den = p_new
        for g in range(MLA_GROUPS):
            a = jnp.exp(m_ref[g] - m_fin)
            num = num + a * acc_ref[g]
            den = den + a * l_ref[g]
        o_lat = num / den
        res = _dot(o_lat.astype(BF16), wuv_ref[...])
        head_of_lane = lax.broadcasted_iota(jnp.int32, (M_HEADS, M_WIDTH), 1) // M_VDIM
        head_of_row = lax.broadcasted_iota(jnp.int32, (M_HEADS, M_WIDTH), 0)
        o_ref[...] = jnp.sum(jnp.where(head_of_lane == head_of_row, res, 0.0), axis=0, keepdims=True)


def _mla_sample(pt_flat, cache_ckv, cache_kpet, q8, ckvn3, kpen3, wuk, wuv, *, dec_batch, n_pages):
    assert n_pages % MLA_PAGES_PER_STEP == 0
    n_steps = n_pages // MLA_PAGES_PER_STEP

    def page_spec(i, shape):
        return pl.BlockSpec((None,) + shape,
                            lambda b, s, pt: (pt[(s * MLA_PAGES_PER_STEP + i) * dec_batch + b], 0, 0))

    per_b = lambda shape: pl.BlockSpec((None,) + shape, lambda b, s, pt: (b, 0, 0))
    grid_spec = pltpu.PrefetchScalarGridSpec(
        num_scalar_prefetch=1,
        grid=(dec_batch, n_steps),
        in_specs=([page_spec(i, (PAGE_SIZE, KV_LORA)) for i in range(MLA_PAGES_PER_STEP)]
                  + [page_spec(i, (M_ROPE, PAGE_SIZE)) for i in range(MLA_PAGES_PER_STEP)]
                  + [per_b((M_HEADS, MQ)), per_b((1, KV_LORA)), per_b((1, M_ROPE)),
                     pl.BlockSpec((KV_LORA, M_HEADS * MQ), lambda b, s, pt: (0, 0)),
                     pl.BlockSpec((KV_LORA, M_WIDTH), lambda b, s, pt: (0, 0))]),
        out_specs=per_b((1, M_WIDTH)),
        scratch_shapes=[pltpu.VMEM((M_HEADS, KV_LORA), F32), pltpu.VMEM((MLA_GROUPS, M_HEADS, 1), F32),
                        pltpu.VMEM((MLA_GROUPS, M_HEADS, 1), F32), pltpu.VMEM((MLA_GROUPS, M_HEADS, KV_LORA), F32)],
    )
    return pl.pallas_call(
        functools.partial(_mla_sample_kernel, n_steps=n_steps),
        grid_spec=grid_spec,
        out_shape=jax.ShapeDtypeStruct((dec_batch, 1, M_WIDTH), F32),
        compiler_params=pltpu.CompilerParams(dimension_semantics=("arbitrary", "arbitrary"),
                                             vmem_limit_bytes=VMEM_LIMIT),
        name="mla_sample",
    )(pt_flat, *([cache_ckv] * MLA_PAGES_PER_STEP), *([cache_kpet] * MLA_PAGES_PER_STEP), q8, ckvn3, kpen3, wuk,
      wuv)


def _rot_half_cols(w):
    half = M_ROPE // 2
    return jnp.concatenate([-w[..., half:], w[..., :half]], axis=-1)


def _prep_weights(w_in, w_q_up, w_kv_up):
    kr = w_in[:, C_KR:C_KR + M_ROPE]
    pad = jnp.zeros((D_MODEL, LANES - M_ROPE), F32)
    w1 = jnp.concatenate([w_in[:, :C_KR], kr, pad, _rot_half_cols(kr), pad], axis=1).astype(BF16)
    wqa = w_in[:, C_QA:C_QA + A_WIDTH]
    wq3 = w_q_up.reshape(Q_LORA, M_HEADS, M_NOPE + M_ROPE)
    nope, rope = wq3[..., :M_NOPE], wq3[..., M_NOPE:]
    z = lambda n: jnp.zeros((Q_LORA, M_HEADS, n), F32)
    wq_a = jnp.concatenate([rope, nope, z(MQ - M_NOPE - M_ROPE)], axis=-1).reshape(Q_LORA, M_HEADS * MQ)
    wq_b = jnp.concatenate([_rot_half_cols(rope), z(MQ - M_ROPE)], axis=-1).reshape(Q_LORA, M_HEADS * MQ)
    wq = jnp.concatenate([wq_a, wq_b], axis=1).astype(BF16)
    wkv3 = w_kv_up.reshape(KV_LORA, M_HEADS, M_NOPE + M_VDIM)
    w_uk, w_uv = wkv3[..., :M_NOPE], wkv3[..., M_NOPE:]
    zk = lambda n: jnp.zeros((KV_LORA, M_HEADS, n), F32)
    wuk = jnp.concatenate([zk(M_ROPE), w_uk, zk(MQ - M_ROPE - M_NOPE)], axis=-1).reshape(KV_LORA, M_HEADS * MQ)
    wuv = w_uv.reshape(KV_LORA, M_WIDTH)
    return w1, wqa, wq, wuk.astype(BF16), wuv.astype(BF16), wuv.T.astype(BF16)


def _rope_tables(pos):
    half = M_ROPE // 2
    inv = ROPE_THETA ** (-jnp.arange(half, dtype=F32) / half)
    ang = pos.astype(F32)[:, None] * inv[None, :]
    n = pos.shape[0]
    cos = jnp.concatenate([jnp.cos(ang), jnp.cos(ang), jnp.ones((n, M_NOPE), F32),
                           jnp.zeros((n, MQ - M_ROPE - M_NOPE), F32)], axis=1)
    sin = jnp.concatenate([jnp.sin(ang), jnp.sin(ang), jnp.zeros((n, MQ - M_ROPE), F32)], axis=1)
    return cos, sin


def kernel(x_prompt, x_sample, cache_moba_k, cache_moba_v, cache_mla_ckv, cache_mla_kpe, page_table, g_attn, w_in,
           g_q, w_q_up, g_kv, w_kv_up, g_out_moba, g_out_mla, w_o, g_ffn, w_gate_up, w_down, g_final):
    batch, seq, _ = x_prompt.shape
    dec_batch, dec_seq, _ = x_sample.shape
    n_pool = cache_moba_k.shape[1]
    n_pages = page_table.shape[1]
    past_len = n_pages * PAGE_SIZE
    assert w_in.shape[0] == 1 and dec_seq == 1 and seq % MOBA_BLOCK == 0 and past_len % MOBA_BLOCK == 0

    w1, wqa, wq, wuk, wuv, wuvt = _prep_weights(w_in[0], w_q_up[0], w_kv_up[0])
    wo = w_o[0].astype(BF16)
    wgu = w_gate_up[0].astype(BF16)
    wd = w_down[0].astype(BF16)
    row2 = lambda a: a.reshape(1, -1)
    slopes = 2.0 ** (-8.0 * jnp.arange(1, A_HEADS + 1, dtype=F32) / A_HEADS)

    cos_p, sin_p = _rope_tables(jnp.arange(seq))
    xp = x_prompt.reshape(batch * seq, D_MODEL)
    (kat, vat, ckv, kpet, qaf, qm, qab, kab, vatb, km, vmtb, ksum) = _project(
        xp, cos_p, sin_p, row2(g_attn[0]), w1, row2(g_q[0]), wq, row2(g_kv[0]), (wuk, wuvt),
        prompt=True, tm=MOBA_BLOCK, seq=seq)
    nblk = seq // MOBA_BLOCK
    oa = _moba_prompt(slopes, qaf, qab, ksum.reshape(batch, nblk, A_WIDTH), kab, vatb, batch=batch, seq=seq)
    om = _mla_prompt(qm, km, vmtb, batch=batch, seq=seq)
    finish_w = (row2(g_out_moba[0]), row2(g_out_mla[0]), wo, row2(g_ffn[0]), wgu, wd, row2(g_final))
    y_prompt = _finish(xp, oa, om, *finish_w, tm=512, name="finish_prompt").reshape(batch, seq, D_MODEL)

    cos_s, sin_s = _rope_tables(jnp.full((dec_batch,), past_len, jnp.int32))
    xs = x_sample.reshape(dec_batch, D_MODEL)
    (kat_s, vat_s, ckv_s, kpet_s, qa_s, qm_s, ka_s, va_s, kpe_s, qat_s) = _project(
        xs, cos_s, sin_s, row2(g_attn[0]), w1, row2(g_q[0]), wq, row2(g_kv[0]), (wqa,),
        prompt=False, tm=dec_batch, seq=dec_batch)
    pt_flat = page_table.T.reshape(-1)
    ckt = jnp.transpose(cache_moba_k[0], (0, 2, 3, 1))
    cvt = jnp.transpose(cache_moba_v[0], (0, 2, 3, 1))
    kpet_cache = jnp.transpose(cache_mla_kpe[0], (0, 2, 1))
    q3 = qa_s.reshape(dec_batch, 1, A_WIDTH)
    knew3 = ka_s.reshape(dec_batch, 1, A_WIDTH)
    vnew3 = va_s.reshape(dec_batch, 1, A_WIDTH)
    sel, scores_s = _moba_gate(pt_flat, ckt.reshape(n_pool, A_WIDTH, PAGE_SIZE), qat_s, q3, knew3,
                               dec_batch=dec_batch, n_pages=n_pages)
    oa_s = _moba_sample(pt_flat, sel[:, :A_HEADS, :MOBA_TOPK].reshape(-1), slopes, cvt, scores_s, q3, knew3, vnew3,
                        dec_batch=dec_batch, n_pages=n_pages)
    om_s = _mla_sample(pt_flat, cache_mla_ckv[0], kpet_cache, qm_s.reshape(dec_batch, M_HEADS, MQ),
                       ckv_s.reshape(dec_batch, 1, KV_LORA), kpe_s.reshape(dec_batch, 1, M_ROPE), wuk, wuv,
                       dec_batch=dec_batch, n_pages=n_pages)
    y_sample = _finish(xs, oa_s.reshape(dec_batch, A_WIDTH), om_s.reshape(dec_batch, M_WIDTH), *finish_w,
                       tm=dec_batch, name="finish_sample").reshape(dec_batch, 1, D_MODEL)

    def heads_last(t_minor, n_tok_batch, n_tok):
        return jnp.transpose(t_minor.reshape(1, n_tok_batch, A_HEADS, A_HEAD_DIM, n_tok), (0, 1, 4, 2, 3))

    k_sample = jnp.transpose(kat_s.reshape(1, 1, A_HEADS, A_HEAD_DIM, dec_batch), (0, 4, 1, 2, 3))
    v_sample = jnp.transpose(vat_s.reshape(1, 1, A_HEADS, A_HEAD_DIM, dec_batch), (0, 4, 1, 2, 3))
    return (y_prompt, y_sample,
            heads_last(kat, batch, seq), heads_last(vat, batch, seq),
            ckv.reshape(1, batch, seq, KV_LORA), jnp.transpose(kpet.reshape(1, batch, M_ROPE, seq), (0, 1, 3, 2)),
            k_sample, v_sample,
            ckv_s.reshape(1, dec_batch, 1, KV_LORA),
            jnp.transpose(kpet_s.reshape(1, 1, M_ROPE, dec_batch), (0, 3, 1, 2)))
```

```python
import functools

import jax
import jax.numpy as jnp
from jax import lax
from jax.experimental import pallas as pl
from jax.experimental.pallas import tpu as pltpu

F32 = jnp.float32
BF16 = jnp.bfloat16

D_MODEL = 1024
A_HEADS = 8
A_HEAD_DIM = 64
A_WIDTH = A_HEADS * A_HEAD_DIM
MOBA_BLOCK = 256
MOBA_TOPK = 3
M_HEADS = 8
M_NOPE = 64
M_ROPE = 32
M_VDIM = 64
M_WIDTH = M_HEADS * M_VDIM
Q_LORA = 256
KV_LORA = 128
ROPE_THETA = 10000.0
D_FF = 2816
PAGE_SIZE = 128
EPS = 1e-6

LANES = 128
SUBLANES = 8
HEAD_PAIRS = A_HEADS // 2
MQ = 128
NEG = -1e30
LOG2E = 1.4426950408889634
VMEM_LIMIT = 56 * 1024 * 1024

C_QA, C_KA, C_VA = 0, A_WIDTH, 2 * A_WIDTH
C_QD = 3 * A_WIDTH
C_KVD = C_QD + Q_LORA
C_KR = C_KVD + KV_LORA
C_KRR = C_KR + LANES
N_W1 = C_KRR + LANES


def _rms(x, g):
    return x * lax.rsqrt(jnp.mean(x * x, axis=-1, keepdims=True) + EPS) * g


def _dot(a, b, precision=None):
    return jnp.dot(a, b, preferred_element_type=F32, precision=precision)


def _dot_nt(a, b, precision=None):
    return lax.dot_general(a, b, (((1,), (1,)), ((), ())), preferred_element_type=F32, precision=precision)


def _proj_kernel(x_ref, cos_ref, sin_ref, ga_ref, w1_ref, gq_ref, wq_ref, gkv_ref, *rest, prompt, tm, qk_scale):
    if prompt:
        (wuk_ref, wuvt_ref, kat_o, vat_o, ckv_o, kpet_o, qaf_o, qm_o, qab_o, kab_o, vatb_o, km_o, vmtb_o,
         ksum_o) = rest
    else:
        wqa_ref, kat_o, vat_o, ckv_o, kpet_o, qaf_o, qm_o, ka_o, va_o, kpe_o, qat_o = rest
    xn = _rms(x_ref[...], ga_ref[...])
    xb = xn.astype(BF16)
    z = _dot(xb, w1_ref[...])
    ka = z[:, C_KA:C_KA + A_WIDTH]
    va = z[:, C_VA:C_VA + A_WIDTH]
    if prompt:
        qa = z[:, C_QA:C_QA + A_WIDTH]
    else:
        qa = _dot(xn, wqa_ref[...], precision=lax.Precision.HIGHEST)
    cos = cos_ref[...]
    sin = sin_ref[...]
    kpe = z[:, C_KR:C_KR + LANES] * cos + z[:, C_KRR:C_KRR + LANES] * sin
    ckv = _rms(z[:, C_KVD:C_KVD + KV_LORA], gkv_ref[...])
    qn = _rms(z[:, C_QD:C_QD + Q_LORA], gq_ref[...]).astype(BF16)
    qab = _dot(qn, wq_ref[...])
    kat = ka.T
    vat = va.T
    kat_o[...] = kat
    vat_o[...] = vat
    ckv_o[...] = ckv
    kpet_o[...] = kpe.T[:M_ROPE]
    qaf_o[...] = qa
    for h in range(M_HEADS):
        sl = slice(h * MQ, (h + 1) * MQ)
        q_h = (qab[:, sl] * cos + qab[:, M_HEADS * MQ + h * MQ:M_HEADS * MQ + (h + 1) * MQ] * sin) * qk_scale
        qm_o[:, sl] = q_h.astype(qm_o.dtype)
    if prompt:
        ckv_b = ckv.astype(BF16)
        qab_o[...] = (qa * (A_HEAD_DIM ** -0.5 * LOG2E)).astype(BF16)
        kab_o[...] = ka.astype(BF16)
        kn = _dot(ckv_b, wuk_ref[...])
        for h in range(M_HEADS):
            sl = slice(h * MQ, (h + 1) * MQ)
            km_o[:, sl] = (kn[:, sl] + kpe).astype(BF16)
        vatb = vat.astype(BF16)
        vmtb = _dot_nt(wuvt_ref[...], ckv_b).astype(BF16)
        for j in range(tm // MOBA_BLOCK):
            rows = slice(j * MOBA_BLOCK, (j + 1) * MOBA_BLOCK)
            vatb_o[j] = vatb[:, rows]
            vmtb_o[j] = vmtb[:, rows]
            ksum_o[j] = jnp.sum(ka[rows], axis=0, keepdims=True)
    else:
        ka_o[...] = ka
        va_o[...] = va
        kpe_o[...] = kpe[:, :M_ROPE]
        qat_o[...] = qa.T


def _const_spec(shape):
    return pl.BlockSpec(shape, lambda *_: (0,) * len(shape))


def _project(x2d, cos_tab, sin_tab, g_attn, w1, g_q, wq, g_kv, extra_weights, *, prompt, tm, seq):
    t = x2d.shape[0]
    assert t % tm == 0 and seq % tm == 0
    n_pos_tiles = seq // tm
    batch = t // seq
    row = lambda i: (i, 0)
    tok_minor = lambda i: (i // n_pos_tiles, 0, i % n_pos_tiles)
    in_specs = [
        pl.BlockSpec((tm, D_MODEL), row),
        pl.BlockSpec((tm, LANES), lambda i: (i % n_pos_tiles, 0)),
        pl.BlockSpec((tm, LANES), lambda i: (i % n_pos_tiles, 0)),
        _const_spec((1, D_MODEL)),
        _const_spec((D_MODEL, N_W1)),
        _const_spec((1, Q_LORA)),
        _const_spec((Q_LORA, 2 * M_HEADS * MQ)),
        _const_spec((1, KV_LORA)),
    ]
    args = [x2d, cos_tab, sin_tab, g_attn, w1, g_q, wq, g_kv]
    f32_out = lambda n: jax.ShapeDtypeStruct((t, n), F32)
    tm_out = lambda n: jax.ShapeDtypeStruct((batch, n, seq), F32)
    out_shape = [tm_out(A_WIDTH), tm_out(A_WIDTH), f32_out(KV_LORA), tm_out(M_ROPE), f32_out(A_WIDTH)]
    out_specs = [pl.BlockSpec((None, A_WIDTH, tm), tok_minor), pl.BlockSpec((None, A_WIDTH, tm), tok_minor),
                 pl.BlockSpec((tm, KV_LORA), row), pl.BlockSpec((None, M_ROPE, tm), tok_minor),
                 pl.BlockSpec((tm, A_WIDTH), row)]
    if prompt:
        assert tm % MOBA_BLOCK == 0
        wuk, wuvt = extra_weights
        nb_tile = tm // MOBA_BLOCK
        nblk = seq // MOBA_BLOCK
        in_specs += [_const_spec((KV_LORA, M_HEADS * MQ)), _const_spec((M_WIDTH, KV_LORA))]
        args += [wuk, wuvt]
        bf_out = lambda n: jax.ShapeDtypeStruct((t, n), BF16)
        vt_shape = jax.ShapeDtypeStruct((batch, nblk, A_WIDTH, MOBA_BLOCK), BF16)
        vt_spec = pl.BlockSpec((None, nb_tile, A_WIDTH, MOBA_BLOCK),
                               lambda i: (i // n_pos_tiles, i % n_pos_tiles, 0, 0))
        out_shape += [bf_out(M_HEADS * MQ), bf_out(A_WIDTH), bf_out(A_WIDTH), vt_shape, bf_out(M_HEADS * MQ),
                      vt_shape, jax.ShapeDtypeStruct((t // MOBA_BLOCK, 1, A_WIDTH), F32)]
        out_specs += [pl.BlockSpec((tm, M_HEADS * MQ), row), pl.BlockSpec((tm, A_WIDTH), row),
                      pl.BlockSpec((tm, A_WIDTH), row), vt_spec, pl.BlockSpec((tm, M_HEADS * MQ), row), vt_spec,
                      pl.BlockSpec((nb_tile, 1, A_WIDTH), lambda i: (i, 0, 0))]
    else:
        assert batch == 1
        (wqa,) = extra_weights
        in_specs += [_const_spec((D_MODEL, A_WIDTH))]
        args += [wqa]
        out_shape += [f32_out(M_HEADS * MQ), f32_out(A_WIDTH), f32_out(A_WIDTH), f32_out(M_ROPE),
                      jax.ShapeDtypeStruct((A_WIDTH, t), F32)]
        out_specs += [pl.BlockSpec((tm, M_HEADS * MQ), row), pl.BlockSpec((tm, A_WIDTH), row),
                      pl.BlockSpec((tm, A_WIDTH), row), pl.BlockSpec((tm, M_ROPE), row),
                      pl.BlockSpec((A_WIDTH, tm), lambda i: (0, i))]
    return pl.pallas_call(
        functools.partial(_proj_kernel, prompt=prompt, tm=tm,
                          qk_scale=(M_NOPE + M_ROPE) ** -0.5 * (LOG2E if prompt else 1.0)),
        grid=(t // tm,),
        in_specs=in_specs,
        out_specs=out_specs,
        out_shape=out_shape,
        compiler_params=pltpu.CompilerParams(dimension_semantics=("arbitrary",), vmem_limit_bytes=VMEM_LIMIT),
        name="proj_prompt" if prompt else "proj_sample",
    )(*args)


ATTN_CHUNK = 2
CHUNKS_PER_STEP = 4


ONES_ROWS = 16


def _run_chains(carry, chains, logits_fn, values_fn):
    blk = MOBA_BLOCK
    carry = list(carry)
    ones = jnp.ones((ONES_ROWS, blk), BF16)
    scores = [logits_fn(*chain) for chain in chains]
    for (c, hh, _), logits in zip(chains, scores):
        m, acc = carry[hh]
        m_new = jnp.maximum(m, jnp.max(logits, axis=0, keepdims=True))
        pb = jnp.exp2(logits - m_new).astype(BF16)
        vt_parts = values_fn(c, hh)
        pv = None
        for t, vt in enumerate(vt_parts):
            part = _dot(jnp.concatenate([vt, ones], axis=0), pb[t * blk:(t + 1) * blk])
            pv = part if pv is None else pv + part
        carry[hh] = (m_new, jnp.exp2(m - m_new) * acc + pv)
    return tuple(carry)


def _attend_causal(qi, chunk, dv, logits_fn, values_fn):
    blk = MOBA_BLOCK
    heads = range(2)
    c_diag = qi // chunk
    n_past_steps = c_diag // CHUNKS_PER_STEP

    def step_chains(i, n_chunks, mask_last):
        return [(i * CHUNKS_PER_STEP + u, hh, mask_last and u == n_chunks - 1)
                for u in range(n_chunks) for hh in heads]

    init = tuple((jnp.full((1, blk), NEG, F32), jnp.zeros((dv + ONES_ROWS, blk), F32)) for _ in heads)
    last_step = [functools.partial(_run_chains, chains=step_chains(n_past_steps, n, True),
                                   logits_fn=logits_fn, values_fn=values_fn)
                 for n in range(1, CHUNKS_PER_STEP + 1)]
    carry = lax.switch(c_diag % CHUNKS_PER_STEP, last_step, init)
    carry = lax.fori_loop(
        0, n_past_steps,
        lambda i, cr: _run_chains(cr, step_chains(i, CHUNKS_PER_STEP, False), logits_fn, values_fn), carry)
    return jnp.concatenate([acc[:dv] / acc[dv:dv + 1] for _, acc in carry], axis=0)


def _chunk_key_minus_query(chunk):
    shape = (chunk * MOBA_BLOCK, MOBA_BLOCK)
    return lax.broadcasted_iota(jnp.int32, shape, 0) - lax.broadcasted_iota(jnp.int32, shape, 1)


def _key_minus_query(blk):
    return (lax.broadcasted_iota(jnp.int32, (blk, blk), 0) - lax.broadcasted_iota(jnp.int32, (blk, blk), 1))


def _top_blocks(gate, blk_id, n_cand):
    picks = []
    g = gate
    for _ in range(MOBA_TOPK):
        top = jnp.max(g, axis=0, keepdims=True)
        first = jnp.min(jnp.where(g == top, blk_id, n_cand), axis=0, keepdims=True)
        picks.append(first)
        g = jnp.where(blk_id == first, -jnp.inf, g)
    return picks


AUG_SLOPE_ROWS = SUBLANES


def _bf16_split3(x):
    hi = x.astype(BF16).astype(F32)
    mid = (x - hi).astype(BF16).astype(F32)
    lo = (x - hi - mid).astype(BF16).astype(F32)
    return hi, mid, lo


def _moba_key_bias_columns(seq):
    nblk = seq // MOBA_BLOCK
    assert AUG_SLOPE_ROWS + 3 * nblk <= LANES
    pos = jnp.arange(seq)
    col = jnp.arange(LANES)[None, :]
    in_blk = (col >= AUG_SLOPE_ROWS) & (col < AUG_SLOPE_ROWS + 3 * nblk)
    blk_of_col = (col - AUG_SLOPE_ROWS) % nblk
    aug = jnp.where(col < 3, (pos % MOBA_BLOCK)[:, None], 0)
    aug = jnp.where(in_blk & (blk_of_col == (pos // MOBA_BLOCK)[:, None]), 1, aug)
    return aug.astype(BF16)


def _moba_query_bias_rows(attended, slope2, qi, nblk):
    blk = MOBA_BLOCK
    blk_id = lax.broadcasted_iota(jnp.int32, (nblk, blk), 0)
    q_off = lax.broadcasted_iota(jnp.int32, (nblk, blk), 1).astype(F32)
    per_block = jnp.where(attended, 0.0, NEG) + slope2 * ((blk_id - qi) * blk).astype(F32) - slope2 * q_off
    row = lax.broadcasted_iota(jnp.int32, (AUG_SLOPE_ROWS, blk), 0)
    s_hi, s_mid, s_lo = _bf16_split3(jnp.full((AUG_SLOPE_ROWS, blk), slope2, F32))
    slope_rows = jnp.where(row == 0, s_hi, jnp.where(row == 1, s_mid, jnp.where(row == 2, s_lo, 0.0)))
    pad = jnp.zeros((LANES - AUG_SLOPE_ROWS - 3 * nblk, blk), F32)
    return jnp.concatenate([slope_rows, *_bf16_split3(per_block), pad], axis=0)


def _moba_prompt_kernel(slopes_ref, qf_ref, qb_ref, ksum_ref, k_ref, kaug_ref, vt_ref, o_ref, *, nblk, chunk):
    blk = MOBA_BLOCK
    ck = chunk * blk
    hp = pl.program_id(1)
    qi = pl.program_id(2)
    lane = lax.broadcasted_iota(jnp.int32, (1, LANES), 1)
    kmean = ksum_ref[...] * (1.0 / blk)
    blk_id = lax.broadcasted_iota(jnp.int32, (nblk, blk), 0)
    past = blk_id < qi
    heads = range(2)
    q = []
    for hh in heads:
        head_lanes = (lane >= A_HEAD_DIM) if hh else (lane < A_HEAD_DIM)
        q_h = jnp.where(head_lanes, qb_ref[...], jnp.zeros((), BF16))
        qf_h = jnp.where(head_lanes, qf_ref[...], 0.0)
        gate = _dot_nt(kmean, qf_h, precision=lax.Precision.HIGHEST)
        attended = blk_id == qi
        for first in _top_blocks(jnp.where(past, gate, -jnp.inf), blk_id, nblk):
            attended = jnp.logical_or(attended, jnp.logical_and(blk_id == first, past))
        bias_rows = _moba_query_bias_rows(attended, slopes_ref[2 * hp + hh] * LOG2E, qi, nblk)
        q.append(jnp.concatenate([q_h, bias_rows.T.astype(BF16)], axis=1))

    rel = _chunk_key_minus_query(chunk)

    def logits(c, hh, masked):
        rows = pl.ds(pl.multiple_of(c * ck, ck), ck)
        keys = jnp.concatenate([k_ref[rows, :], kaug_ref[rows, :]], axis=1)
        s = _dot_nt(keys, q[hh])
        return jnp.where(rel <= (qi - c * chunk) * blk, s, NEG) if masked else s

    def values(c, hh):
        return [vt_ref[c * chunk + t, hh * A_HEAD_DIM:(hh + 1) * A_HEAD_DIM, :] for t in range(chunk)]

    o_ref[...] = _attend_causal(qi, chunk, A_HEAD_DIM, logits, values).T


def _moba_prompt(slopes, qaf, qab, ksum, kab, vatb, *, batch, seq):
    nblk = seq // MOBA_BLOCK
    blk = MOBA_BLOCK
    assert nblk % (ATTN_CHUNK * CHUNKS_PER_STEP) == 0
    q_spec = pl.BlockSpec((blk, LANES), lambda b, hp, qi: (b * nblk + qi, hp))
    return pl.pallas_call(
        functools.partial(_moba_prompt_kernel, nblk=nblk, chunk=ATTN_CHUNK),
        grid=(batch, HEAD_PAIRS, nblk),
        in_specs=[
            pl.BlockSpec(memory_space=pltpu.SMEM),
            q_spec,
            q_spec,
            pl.BlockSpec((None, nblk, LANES), lambda b, hp, qi: (b, 0, hp)),
            pl.BlockSpec((seq, LANES), lambda b, hp, qi: (b, hp)),
            pl.BlockSpec((seq, LANES), lambda b, hp, qi: (0, 0)),
            pl.BlockSpec((None, nblk, LANES, blk), lambda b, hp, qi: (b, 0, hp, 0)),
        ],
        out_specs=q_spec,
        out_shape=jax.ShapeDtypeStruct((batch * seq, A_WIDTH), F32),
        compiler_params=pltpu.CompilerParams(dimension_semantics=("arbitrary",) * 3, vmem_limit_bytes=VMEM_LIMIT),
        name="moba_prompt",
    )(slopes, qaf, qab, ksum, kab, _moba_key_bias_columns(seq), vatb)


GATE_UNITS_PER_STEP = 2


def _mla_prompt_kernel(pt_ref, *refs, chunk, nblk, gate_steps, gate_nblk):
    n_gate_pages = GATE_UNITS_PER_STEP * PAGES_PER_STEP
    pages = refs[:n_gate_pages]
    (qat_ref, qs_ref, knew_ref, q_ref, k_ref, vt_ref, o_ref, sel_ref, score_ref, qb_ref,
     part_ref) = refs[n_gate_pages:]
    step = (pl.program_id(0) * HEAD_PAIRS + pl.program_id(1)) * nblk + pl.program_id(2)
    steps_per_seq = gate_steps // GATE_UNITS_PER_STEP
    for u in range(GATE_UNITS_PER_STEP):
        _moba_gate_unit(step // steps_per_seq, (step % steps_per_seq) * GATE_UNITS_PER_STEP + u,
                        pages[u * PAGES_PER_STEP:(u + 1) * PAGES_PER_STEP], u * PAGES_PER_STEP,
                        qat_ref, qs_ref, knew_ref, sel_ref, score_ref, qb_ref, part_ref,
                        n_steps=gate_steps, nblk=gate_nblk)

    blk = MOBA_BLOCK
    ck = chunk * blk
    qi = pl.program_id(2)
    heads = range(2)
    q = [q_ref[:, hh * MQ:(hh + 1) * MQ] for hh in heads]

    rel = _chunk_key_minus_query(chunk)

    def logits(c, hh, masked):
        rows = pl.ds(pl.multiple_of(c * ck, ck), ck)
        s = _dot_nt(k_ref[rows, hh * MQ:(hh + 1) * MQ], q[hh])
        return jnp.where(rel <= (qi - c * chunk) * blk, s, NEG) if masked else s

    def values(c, hh):
        return [vt_ref[c * chunk + t, hh * M_VDIM:(hh + 1) * M_VDIM, :] for t in range(chunk)]

    o_ref[...] = _attend_causal(qi, chunk, M_VDIM, logits, values).T


def _mla_prompt_and_gate(qm, km, vmtb, pt_flat, cache_kt, qat, q3, knew3, *, batch, seq, dec_batch, n_pages):
    nblk = seq // MOBA_BLOCK
    blk = MOBA_BLOCK
    gate_nblk = n_pages // PAGES_PER_BLOCK
    gate_steps = n_pages // PAGES_PER_STEP
    pages_per_grid_step = GATE_UNITS_PER_STEP * PAGES_PER_STEP
    steps_per_seq = gate_steps // GATE_UNITS_PER_STEP
    assert nblk % (ATTN_CHUNK * CHUNKS_PER_STEP) == 0
    assert n_pages % pages_per_grid_step == 0 and gate_nblk >= MOBA_TOPK and dec_batch <= LANES
    assert batch * HEAD_PAIRS * nblk == dec_batch * steps_per_seq

    def step_of(b, hp, qi):
        return (b * HEAD_PAIRS + hp) * nblk + qi

    def page_spec(i):
        def index_map(b, hp, qi, pt):
            t = step_of(b, hp, qi)
            page = (t % steps_per_seq) * pages_per_grid_step + i
            return (pt[page * dec_batch + t // steps_per_seq], 0, 0)
        return pl.BlockSpec((None, A_WIDTH, PAGE_SIZE), index_map)

    seq_vec = pl.BlockSpec((None, 1, A_WIDTH), lambda b, hp, qi, pt: (step_of(b, hp, qi) // steps_per_seq, 0, 0))
    grid_spec = pltpu.PrefetchScalarGridSpec(
        num_scalar_prefetch=1,
        grid=(batch, HEAD_PAIRS, nblk),
        in_specs=([page_spec(i) for i in range(pages_per_grid_step)] + [
            pl.BlockSpec((A_WIDTH, dec_batch), lambda b, hp, qi, pt: (0, 0)),
            seq_vec,
            seq_vec,
            pl.BlockSpec((blk, 2 * MQ), lambda b, hp, qi, pt: (b * nblk + qi, hp)),
            pl.BlockSpec((seq, 2 * MQ), lambda b, hp, qi, pt: (b, hp)),
            pl.BlockSpec((None, nblk, LANES, blk), lambda b, hp, qi, pt: (b, 0, hp, 0)),
        ]),
        out_specs=[
            pl.BlockSpec((blk, LANES), lambda b, hp, qi, pt: (b * nblk + qi, hp)),
            pl.BlockSpec((None, SUBLANES, LANES), lambda b, hp, qi, pt: (step_of(b, hp, qi) // steps_per_seq, 0, 0)),
            pl.BlockSpec((None, pages_per_grid_step, A_HEADS, PAGE_SIZE),
                         lambda b, hp, qi, pt: (step_of(b, hp, qi) // steps_per_seq,
                                                step_of(b, hp, qi) % steps_per_seq, 0, 0)),
        ],
        scratch_shapes=[pltpu.VMEM((A_WIDTH, LANES), F32), pltpu.VMEM((gate_nblk * A_HEADS, LANES), F32)],
    )
    return pl.pallas_call(
        functools.partial(_mla_prompt_kernel, chunk=ATTN_CHUNK, nblk=nblk, gate_steps=gate_steps,
                          gate_nblk=gate_nblk),
        grid_spec=grid_spec,
        out_shape=[jax.ShapeDtypeStruct((batch * seq, M_WIDTH), F32),
                   jax.ShapeDtypeStruct((dec_batch, SUBLANES, LANES), jnp.int32),
                   jax.ShapeDtypeStruct((dec_batch, n_pages, A_HEADS, PAGE_SIZE), F32)],
        compiler_params=pltpu.CompilerParams(dimension_semantics=("arbitrary",) * 3, vmem_limit_bytes=VMEM_LIMIT),
        name="mla_prompt_moba_gate",
    )(pt_flat, *([cache_kt] * pages_per_grid_step), qat, q3, knew3, qm, km, vmtb)


def _finish_kernel(x_ref, oa_ref, om_ref, goa_ref, gom_ref, wo_ref, gf_ref, wgu_ref, wd_ref, gfin_ref, y_ref):
    oa = _rms(oa_ref[...], goa_ref[...])
    om = _rms(om_ref[...], gom_ref[...])
    mix = jnp.concatenate([oa, om], axis=-1).astype(BF16)
    h = x_ref[...] + _dot(mix, wo_ref[...])
    hn = _rms(h, gf_ref[...]).astype(BF16)
    gu = _dot(hn, wgu_ref[...])
    g = gu[:, :D_FF]
    u = gu[:, D_FF:]
    act = (g / (1.0 + jnp.exp(-g)) * u).astype(BF16)
    h = h + _dot(act, wd_ref[...])
    y_ref[...] = _rms(h, gfin_ref[...])


def _finish(x2d, oa, om, g_oa, g_om, wo, g_ffn, wgu, wd, g_final, *, tm, name):
    t = x2d.shape[0]
    assert t % tm == 0
    row = lambda i: (i, 0)
    single = pl.Buffered(1)
    wspec = lambda shape: pl.BlockSpec(shape, lambda i: (0, 0), pipeline_mode=single)
    return pl.pallas_call(
        _finish_kernel,
        grid=(t // tm,),
        in_specs=[
            pl.BlockSpec((tm, D_MODEL), row),
            pl.BlockSpec((tm, A_WIDTH), row),
            pl.BlockSpec((tm, M_WIDTH), row),
            _const_spec((1, A_WIDTH)),
            _const_spec((1, M_WIDTH)),
            wspec((A_WIDTH + M_WIDTH, D_MODEL)),
            _const_spec((1, D_MODEL)),
            wspec((D_MODEL, 2 * D_FF)),
            wspec((D_FF, D_MODEL)),
            _const_spec((1, D_MODEL)),
        ],
        out_specs=pl.BlockSpec((tm, D_MODEL), row),
        out_shape=jax.ShapeDtypeStruct((t, D_MODEL), F32),
        compiler_params=pltpu.CompilerParams(dimension_semantics=("arbitrary",), vmem_limit_bytes=VMEM_LIMIT),
        name=name,
    )(x2d, oa, om, g_oa, g_om, wo, g_ffn, wgu, wd, g_final)


PAGES_PER_STEP = 16
PAGES_PER_BLOCK = MOBA_BLOCK // PAGE_SIZE
BLOCKS_PER_STEP = PAGES_PER_STEP // PAGES_PER_BLOCK


def _moba_gate_unit(b, s, pages, score_page0, qat_ref, q_ref, knew_ref, sel_ref, score_ref, qb_ref, part_ref, *,
                    n_steps, nblk):
    lane = lax.broadcasted_iota(jnp.int32, (1, LANES), 1)

    @pl.when(s == 0)
    def _():
        seq_id = lax.broadcasted_iota(jnp.int32, (1, qat_ref.shape[1]), 1)
        col = jnp.sum(jnp.where(seq_id == b, qat_ref[...], 0.0), axis=1, keepdims=True)
        qb_ref[...] = jnp.broadcast_to(col, qb_ref.shape)

    qb = qb_ref[...]
    for i in range(BLOCKS_PER_STEP):
        tiles = []
        for pg in range(PAGES_PER_BLOCK):
            prod = pages[PAGES_PER_BLOCK * i + pg][...] * qb
            tiles.append(jnp.concatenate(
                [jnp.sum(prod[h * A_HEAD_DIM:(h + 1) * A_HEAD_DIM], axis=0, keepdims=True)
                 for h in range(A_HEADS)], axis=0))
            score_ref[score_page0 + PAGES_PER_BLOCK * i + pg] = tiles[pg]
        part_ref[pl.ds(pl.multiple_of((s * BLOCKS_PER_STEP + i) * A_HEADS, A_HEADS), A_HEADS), :] = (
            functools.reduce(jnp.add, tiles))

    @pl.when(s == n_steps - 1)
    def _():
        gate = jnp.sum(part_ref[...], axis=1, keepdims=True) * (1.0 / MOBA_BLOCK)
        prod_new = knew_ref[...] * q_ref[...]
        lane_head = lax.broadcasted_iota(jnp.int32, (1, A_WIDTH), 1) // A_HEAD_DIM
        head_row = lax.broadcasted_iota(jnp.int32, (A_HEADS, 1), 0)
        g_new = jnp.zeros((A_HEADS, 1), F32)
        for h in range(A_HEADS):
            g_h = jnp.sum(jnp.where(lane_head == h, prod_new, 0.0), axis=1, keepdims=True)
            g_new = jnp.where(head_row == h, g_h * (1.0 / MOBA_BLOCK), g_new)
        n_cand = nblk + 1
        cand = [gate[j * A_HEADS:(j + 1) * A_HEADS] for j in range(nblk)] + [g_new]
        cand = [g if j < nblk else jnp.full_like(g, -jnp.inf) for j, g in enumerate(cand)]
        out = jnp.zeros((A_HEADS, LANES), jnp.int32)
        for r in range(MOBA_TOPK):
            top = functools.reduce(jnp.maximum, cand)
            first = functools.reduce(jnp.minimum, [jnp.where(g == top, j, n_cand) for j, g in enumerate(cand)])
            out = jnp.where(lane == r, first, out)
            cand = [jnp.where(first == j, -jnp.inf, g) for j, g in enumerate(cand)]
        sel_ref[...] = out


SLABS = 2 * MOBA_TOPK * PAGES_PER_BLOCK


def _moba_sample_kernel(pt_ref, sel_ref, slopes_ref, *refs, past_len):
    v_refs = refs[:SLABS]
    score_ref, q_ref, knew_ref, vnew_ref, o_ref = refs[SLABS:]
    b = pl.program_id(0)
    hp = pl.program_id(1)
    n_keys = MOBA_TOPK * MOBA_BLOCK
    slabs_per_head = MOBA_TOPK * PAGES_PER_BLOCK
    lane = lax.broadcasted_iota(jnp.int32, (1, n_keys), 1)
    scale = A_HEAD_DIM ** -0.5
    outs = []
    for hh in range(2):
        h = 2 * hp + hh
        cols = slice(hh * A_HEAD_DIM, (hh + 1) * A_HEAD_DIM)
        kpos = lane % MOBA_BLOCK
        pieces = []
        for r in range(MOBA_TOPK):
            blk = sel_ref[(b * A_HEADS + h) * MOBA_TOPK + r]
            kpos = kpos + jnp.where(lane // MOBA_BLOCK == r, blk * MOBA_BLOCK, 0)
            pieces += [score_ref[blk * PAGES_PER_BLOCK + pg, pl.ds(h, 1), :] for pg in range(PAGES_PER_BLOCK)]
        logits = jnp.concatenate(pieces, axis=1) * scale - slopes_ref[h] * (past_len - kpos).astype(F32)
        s_new = jnp.sum(knew_ref[:, cols] * q_ref[:, cols], axis=1, keepdims=True) * scale
        m = jnp.maximum(s_new, jnp.max(logits, axis=1, keepdims=True))
        p_new = jnp.exp(s_new - m)
        p = jnp.exp(logits - m)
        l = p_new + jnp.sum(p, axis=1, keepdims=True)
        vt = jnp.concatenate([v_refs[hh * slabs_per_head + i][...] for i in range(slabs_per_head)], axis=1)
        pv = _dot_nt(jnp.broadcast_to(p, (SUBLANES, n_keys)).astype(BF16), vt.astype(BF16))[:1]
        outs.append((p_new * vnew_ref[:, cols] + pv) / l)
    o_ref[...] = jnp.concatenate(outs, axis=1)


def _moba_sample(pt_flat, sel_flat, slopes, cache_vt, scores, q3, knew3, vnew3, *, dec_batch, n_pages):
    def slab_spec(slab):
        hh, rem = divmod(slab, MOBA_TOPK * PAGES_PER_BLOCK)
        r, pg = divmod(rem, PAGES_PER_BLOCK)

        def index_map(b, hp, pt, sel):
            blk = sel[(b * A_HEADS + 2 * hp + hh) * MOBA_TOPK + r]
            return (pt[(blk * PAGES_PER_BLOCK + pg) * dec_batch + b], 2 * hp + hh, 0, 0)

        return pl.BlockSpec((None, None, A_HEAD_DIM, PAGE_SIZE), index_map)

    vec_spec = pl.BlockSpec((None, 1, LANES), lambda b, hp, pt, sel: (b, 0, hp))
    slabs = [slab_spec(i) for i in range(SLABS)]
    grid_spec = pltpu.PrefetchScalarGridSpec(
        num_scalar_prefetch=2,
        grid=(dec_batch, HEAD_PAIRS),
        in_specs=([pl.BlockSpec(memory_space=pltpu.SMEM)] + slabs
                  + [pl.BlockSpec((None, n_pages, A_HEADS, PAGE_SIZE), lambda b, hp, pt, sel: (b, 0, 0, 0)),
                     vec_spec, vec_spec, vec_spec]),
        out_specs=vec_spec,
    )
    return pl.pallas_call(
        functools.partial(_moba_sample_kernel, past_len=n_pages * PAGE_SIZE),
        grid_spec=grid_spec,
        out_shape=jax.ShapeDtypeStruct((dec_batch, 1, A_WIDTH), F32),
        compiler_params=pltpu.CompilerParams(dimension_semantics=("arbitrary", "arbitrary"),
                                             vmem_limit_bytes=VMEM_LIMIT),
        name="moba_sample_attn",
    )(pt_flat, sel_flat, slopes, *([cache_vt] * SLABS), scores, q3, knew3, vnew3)


MLA_GROUPS = 4
MLA_GROUP_PAGES = 8
MLA_PAGES_PER_STEP = MLA_GROUPS * MLA_GROUP_PAGES


def _mla_sample_kernel(pt_ref, *refs, n_steps):
    ckv_pages = refs[:MLA_PAGES_PER_STEP]
    kpe_pages = refs[MLA_PAGES_PER_STEP:2 * MLA_PAGES_PER_STEP]
    (q_ref, ckvn_ref, kpen_ref, wuk_ref, wuv_ref, o_ref, qlat_ref, m_ref, l_ref,
     acc_ref) = refs[2 * MLA_PAGES_PER_STEP:]
    s = pl.program_id(1)
    q8 = q_ref[...]
    q_pe = q8[:, :M_ROPE]

    @pl.when(s == 0)
    def _():
        head_of_lane = lax.broadcasted_iota(jnp.int32, (M_HEADS, M_HEADS * MQ), 1) // MQ
        head_of_row = lax.broadcasted_iota(jnp.int32, (M_HEADS, M_HEADS * MQ), 0)
        q_bd = jnp.where(head_of_lane == head_of_row, jnp.concatenate([q8] * M_HEADS, axis=1), 0.0)
        qlat_ref[...] = _dot_nt(q_bd.astype(BF16), wuk_ref[...])
        m_ref[...] = jnp.full(m_ref.shape, NEG, F32)
        l_ref[...] = jnp.zeros(l_ref.shape, F32)
        acc_ref[...] = jnp.zeros(acc_ref.shape, F32)

    qlat = qlat_ref[...]
    qlat_b = qlat.astype(BF16)
    qpe_b = q_pe.astype(BF16)
    pcs, scores = [], []
    for g in range(MLA_GROUPS):
        pages = range(g * MLA_GROUP_PAGES, (g + 1) * MLA_GROUP_PAGES)
        pcs.append(jnp.concatenate([ckv_pages[i][...].astype(BF16) for i in pages], axis=0))
        kpt = jnp.concatenate([kpe_pages[i][...].astype(BF16) for i in pages], axis=1)
        scores.append(_dot_nt(qlat_b, pcs[g]) + _dot(qpe_b, kpt))
    for g in range(MLA_GROUPS):
        m_old = m_ref[g]
        m_new = jnp.maximum(m_old, jnp.max(scores[g], axis=1, keepdims=True))
        alpha = jnp.exp(m_old - m_new)
        p = jnp.exp(scores[g] - m_new)
        m_ref[g] = m_new
        l_ref[g] = alpha * l_ref[g] + jnp.sum(p, axis=1, keepdims=True)
        acc_ref[g] = alpha * acc_ref[g] + _dot(p.astype(BF16), pcs[g])

    @pl.when(s == n_steps - 1)
    def _():
        ckvn = ckvn_ref[...]
        s_new = (jnp.sum(qlat * ckvn, axis=1, keepdims=True)
                 + jnp.sum(q_pe * kpen_ref[...], axis=1, keepdims=True))
        m_fin = s_new
        for g in range(MLA_GROUPS):
            m_fin = jnp.maximum(m_fin, m_ref[g])
        p_new = jnp.exp(s_new - m_fin)
        num = p_new * ckvn
        den = p_new
        for g in range(MLA_GROUPS):
            a = jnp.exp(m_ref[g] - m_fin)
            num = num + a * acc_ref[g]
            den = den + a * l_ref[g]
        o_lat = num / den
        res = _dot(o_lat.astype(BF16), wuv_ref[...])
        head_of_lane = lax.broadcasted_iota(jnp.int32, (M_HEADS, M_WIDTH), 1) // M_VDIM
        head_of_row = lax.broadcasted_iota(jnp.int32, (M_HEADS, M_WIDTH), 0)
        o_ref[...] = jnp.sum(jnp.where(head_of_lane == head_of_row, res, 0.0), axis=0, keepdims=True)


def _mla_sample(pt_flat, cache_ckv, cache_kpet, q8, ckvn3, kpen3, wuk, wuv, *, dec_batch, n_pages):
    assert n_pages % MLA_PAGES_PER_STEP == 0
    n_steps = n_pages // MLA_PAGES_PER_STEP

    def page_spec(i, shape):
        return pl.BlockSpec((None,) + shape,
                            lambda b, s, pt: (pt[(s * MLA_PAGES_PER_STEP + i) * dec_batch + b], 0, 0))

    per_b = lambda shape: pl.BlockSpec((None,) + shape, lambda b, s, pt: (b, 0, 0))
    grid_spec = pltpu.PrefetchScalarGridSpec(
        num_scalar_prefetch=1,
        grid=(dec_batch, n_steps),
        in_specs=([page_spec(i, (PAGE_SIZE, KV_LORA)) for i in range(MLA_PAGES_PER_STEP)]
                  + [page_spec(i, (M_ROPE, PAGE_SIZE)) for i in range(MLA_PAGES_PER_STEP)]
                  + [per_b((M_HEADS, MQ)), per_b((1, KV_LORA)), per_b((1, M_ROPE)),
                     pl.BlockSpec((KV_LORA, M_HEADS * MQ), lambda b, s, pt: (0, 0)),
                     pl.BlockSpec((KV_LORA, M_WIDTH), lambda b, s, pt: (0, 0))]),
        out_specs=per_b((1, M_WIDTH)),
        scratch_shapes=[pltpu.VMEM((M_HEADS, KV_LORA), F32), pltpu.VMEM((MLA_GROUPS, M_HEADS, 1), F32),
                        pltpu.VMEM((MLA_GROUPS, M_HEADS, 1), F32), pltpu.VMEM((MLA_GROUPS, M_HEADS, KV_LORA), F32)],
    )
    return pl.pallas_call(
        functools.partial(_mla_sample_kernel, n_steps=n_steps),
        grid_spec=grid_spec,
        out_shape=jax.ShapeDtypeStruct((dec_batch, 1, M_WIDTH), F32),
        compiler_params=pltpu.CompilerParams(dimension_semantics=("arbitrary", "arbitrary"),
                                             vmem_limit_bytes=VMEM_LIMIT),
        name="mla_sample",
    )(pt_flat, *([cache_ckv] * MLA_PAGES_PER_STEP), *([cache_kpet] * MLA_PAGES_PER_STEP), q8, ckvn3, kpen3, wuk,
      wuv)


def _rot_half_cols(w):
    half = M_ROPE // 2
    return jnp.concatenate([-w[..., half:], w[..., :half]], axis=-1)


def _prep_weights(w_in, w_q_up, w_kv_up):
    kr = w_in[:, C_KR:C_KR + M_ROPE]
    pad = jnp.zeros((D_MODEL, LANES - M_ROPE), F32)
    w1 = jnp.concatenate([w_in[:, :C_KR], kr, pad, _rot_half_cols(kr), pad], axis=1).astype(BF16)
    wqa = w_in[:, C_QA:C_QA + A_WIDTH]
    wq3 = w_q_up.reshape(Q_LORA, M_HEADS, M_NOPE + M_ROPE)
    nope, rope = wq3[..., :M_NOPE], wq3[..., M_NOPE:]
    z = lambda n: jnp.zeros((Q_LORA, M_HEADS, n), F32)
    wq_a = jnp.concatenate([rope, nope, z(MQ - M_NOPE - M_ROPE)], axis=-1).reshape(Q_LORA, M_HEADS * MQ)
    wq_b = jnp.concatenate([_rot_half_cols(rope), z(MQ - M_ROPE)], axis=-1).reshape(Q_LORA, M_HEADS * MQ)
    wq = jnp.concatenate([wq_a, wq_b], axis=1).astype(BF16)
    wkv3 = w_kv_up.reshape(KV_LORA, M_HEADS, M_NOPE + M_VDIM)
    w_uk, w_uv = wkv3[..., :M_NOPE], wkv3[..., M_NOPE:]
    zk = lambda n: jnp.zeros((KV_LORA, M_HEADS, n), F32)
    wuk = jnp.concatenate([zk(M_ROPE), w_uk, zk(MQ - M_ROPE - M_NOPE)], axis=-1).reshape(KV_LORA, M_HEADS * MQ)
    wuv = w_uv.reshape(KV_LORA, M_WIDTH)
    return w1, wqa, wq, wuk.astype(BF16), wuv.astype(BF16), wuv.T.astype(BF16)


def _rope_tables(pos):
    half = M_ROPE // 2
    inv = ROPE_THETA ** (-jnp.arange(half, dtype=F32) / half)
    ang = pos.astype(F32)[:, None] * inv[None, :]
    n = pos.shape[0]
    cos = jnp.concatenate([jnp.cos(ang), jnp.cos(ang), jnp.ones((n, M_NOPE), F32),
                           jnp.zeros((n, MQ - M_ROPE - M_NOPE), F32)], axis=1)
    sin = jnp.concatenate([jnp.sin(ang), jnp.sin(ang), jnp.zeros((n, MQ - M_ROPE), F32)], axis=1)
    return cos, sin


def kernel(x_prompt, x_sample, cache_moba_k, cache_moba_v, cache_mla_ckv, cache_mla_kpe, page_table, g_attn, w_in,
           g_q, w_q_up, g_kv, w_kv_up, g_out_moba, g_out_mla, w_o, g_ffn, w_gate_up, w_down, g_final):
    batch, seq, _ = x_prompt.shape
    dec_batch, dec_seq, _ = x_sample.shape
    n_pool = cache_moba_k.shape[1]
    n_pages = page_table.shape[1]
    past_len = n_pages * PAGE_SIZE
    assert w_in.shape[0] == 1 and dec_seq == 1 and seq % MOBA_BLOCK == 0 and past_len % MOBA_BLOCK == 0

    w1, wqa, wq, wuk, wuv, wuvt = _prep_weights(w_in[0], w_q_up[0], w_kv_up[0])
    wo = w_o[0].astype(BF16)
    wgu = w_gate_up[0].astype(BF16)
    wd = w_down[0].astype(BF16)
    row2 = lambda a: a.reshape(1, -1)
    slopes = 2.0 ** (-8.0 * jnp.arange(1, A_HEADS + 1, dtype=F32) / A_HEADS)

    cos_p, sin_p = _rope_tables(jnp.arange(seq))
    xp = x_prompt.reshape(batch * seq, D_MODEL)
    (kat, vat, ckv, kpet, qaf, qm, qab, kab, vatb, km, vmtb, ksum) = _project(
        xp, cos_p, sin_p, row2(g_attn[0]), w1, row2(g_q[0]), wq, row2(g_kv[0]), (wuk, wuvt),
        prompt=True, tm=MOBA_BLOCK, seq=seq)
    cos_s, sin_s = _rope_tables(jnp.full((dec_batch,), past_len, jnp.int32))
    xs = x_sample.reshape(dec_batch, D_MODEL)
    (kat_s, vat_s, ckv_s, kpet_s, qa_s, qm_s, ka_s, va_s, kpe_s, qat_s) = _project(
        xs, cos_s, sin_s, row2(g_attn[0]), w1, row2(g_q[0]), wq, row2(g_kv[0]), (wqa,),
        prompt=False, tm=dec_batch, seq=dec_batch)
    pt_flat = page_table.T.reshape(-1)
    ckt = jnp.transpose(cache_moba_k[0], (0, 2, 3, 1))
    cvt = jnp.transpose(cache_moba_v[0], (0, 2, 3, 1))
    kpet_cache = jnp.transpose(cache_mla_kpe[0], (0, 2, 1))
    q3 = qa_s.reshape(dec_batch, 1, A_WIDTH)
    knew3 = ka_s.reshape(dec_batch, 1, A_WIDTH)
    vnew3 = va_s.reshape(dec_batch, 1, A_WIDTH)

    nblk = seq // MOBA_BLOCK
    oa = _moba_prompt(slopes, qaf, qab, ksum.reshape(batch, nblk, A_WIDTH), kab, vatb, batch=batch, seq=seq)
    om, sel, scores_s = _mla_prompt_and_gate(
        qm, km, vmtb, pt_flat, ckt.reshape(n_pool, A_WIDTH, PAGE_SIZE), qat_s, q3, knew3,
        batch=batch, seq=seq, dec_batch=dec_batch, n_pages=n_pages)
    finish_w = (row2(g_out_moba[0]), row2(g_out_mla[0]), wo, row2(g_ffn[0]), wgu, wd, row2(g_final))
    y_prompt = _finish(xp, oa, om, *finish_w, tm=512, name="finish_prompt").reshape(batch, seq, D_MODEL)

    oa_s = _moba_sample(pt_flat, sel[:, :A_HEADS, :MOBA_TOPK].reshape(-1), slopes, cvt, scores_s, q3, knew3, vnew3,
                        dec_batch=dec_batch, n_pages=n_pages)
    om_s = _mla_sample(pt_flat, cache_mla_ckv[0], kpet_cache, qm_s.reshape(dec_batch, M_HEADS, MQ),
                       ckv_s.reshape(dec_batch, 1, KV_LORA), kpe_s.reshape(dec_batch, 1, M_ROPE), wuk, wuv,
                       dec_batch=dec_batch, n_pages=n_pages)
    y_sample = _finish(xs, oa_s.reshape(dec_batch, A_WIDTH), om_s.reshape(dec_batch, M_WIDTH), *finish_w,
                       tm=dec_batch, name="finish_sample").reshape(dec_batch, 1, D_MODEL)

    def heads_last(t_minor, n_tok_batch, n_tok):
        return jnp.transpose(t_minor.reshape(1, n_tok_batch, A_HEADS, A_HEAD_DIM, n_tok), (0, 1, 4, 2, 3))

    k_sample = jnp.transpose(kat_s.reshape(1, 1, A_HEADS, A_HEAD_DIM, dec_batch), (0, 4, 1, 2, 3))
    v_sample = jnp.transpose(vat_s.reshape(1, 1, A_HEADS, A_HEAD_DIM, dec_batch), (0, 4, 1, 2, 3))
    return (y_prompt, y_sample,
            heads_last(kat, batch, seq), heads_last(vat, batch, seq),
            ckv.reshape(1, batch, seq, KV_LORA), jnp.transpose(kpet.reshape(1, batch, M_ROPE, seq), (0, 1, 3, 2)),
            k_sample, v_sample,
            ckv_s.reshape(1, dec_batch, 1, KV_LORA),
            jnp.transpose(kpet_s.reshape(1, 1, M_ROPE, dec_batch), (0, 3, 1, 2)))
```

```python
import functools

import jax
import jax.numpy as jnp
from jax import lax
from jax.experimental import pallas as pl
from jax.experimental.pallas import tpu as pltpu

F32 = jnp.float32
BF16 = jnp.bfloat16

D_MODEL = 1024
A_HEADS = 8
A_HEAD_DIM = 64
A_WIDTH = A_HEADS * A_HEAD_DIM
MOBA_BLOCK = 256
MOBA_TOPK = 3
M_HEADS = 8
M_NOPE = 64
M_ROPE = 32
M_VDIM = 64
M_WIDTH = M_HEADS * M_VDIM
Q_LORA = 256
KV_LORA = 128
ROPE_THETA = 10000.0
D_FF = 2816
PAGE_SIZE = 128
EPS = 1e-6

LANES = 128
SUBLANES = 8
HEAD_PAIRS = A_HEADS // 2
MQ = 128
NEG = -1e30
LOG2E = 1.4426950408889634
VMEM_LIMIT = 56 * 1024 * 1024

C_QA, C_KA, C_VA = 0, A_WIDTH, 2 * A_WIDTH
C_QD = 3 * A_WIDTH
C_KVD = C_QD + Q_LORA
C_KR = C_KVD + KV_LORA
C_KRR = C_KR + LANES
N_W1 = C_KRR + LANES


def _rms(x, g):
    return x * lax.rsqrt(jnp.mean(x * x, axis=-1, keepdims=True) + EPS) * g


def _dot(a, b, precision=None):
    return jnp.dot(a, b, preferred_element_type=F32, precision=precision)


def _dot_nt(a, b, precision=None):
    return lax.dot_general(a, b, (((1,), (1,)), ((), ())), preferred_element_type=F32, precision=precision)


def _proj_kernel(x_ref, cos_ref, sin_ref, ga_ref, w1_ref, gq_ref, wq_ref, gkv_ref, *rest, prompt, tm, qk_scale,
                 nblk):
    if prompt:
        (wuk_ref, wuvt_ref, slopes_ref, kat_o, vat_o, ckv_o, kpet_o, qm_o, qab_o, kab_o, vatb_o, km_o, vmtb_o,
         qaug_o, ksum_ref) = rest
    else:
        wqa_ref, kat_o, vat_o, ckv_o, kpet_o, qm_o, qaf_o, ka_o, va_o, kpe_o, qat_o = rest
    xn = _rms(x_ref[...], ga_ref[...])
    xb = xn.astype(BF16)
    z = _dot(xb, w1_ref[...])
    ka = z[:, C_KA:C_KA + A_WIDTH]
    va = z[:, C_VA:C_VA + A_WIDTH]
    if prompt:
        qa = z[:, C_QA:C_QA + A_WIDTH]
    else:
        qa = _dot(xn, wqa_ref[...], precision=lax.Precision.HIGHEST)
    cos = cos_ref[...]
    sin = sin_ref[...]
    kpe = z[:, C_KR:C_KR + LANES] * cos + z[:, C_KRR:C_KRR + LANES] * sin
    ckv = _rms(z[:, C_KVD:C_KVD + KV_LORA], gkv_ref[...])
    qn = _rms(z[:, C_QD:C_QD + Q_LORA], gq_ref[...]).astype(BF16)
    qab = _dot(qn, wq_ref[...])
    kat = ka.T
    vat = va.T
    kat_o[...] = kat
    vat_o[...] = vat
    ckv_o[...] = ckv
    kpet_o[...] = kpe.T[:M_ROPE]
    for h in range(M_HEADS):
        sl = slice(h * MQ, (h + 1) * MQ)
        q_h = (qab[:, sl] * cos + qab[:, M_HEADS * MQ + h * MQ:M_HEADS * MQ + (h + 1) * MQ] * sin) * qk_scale
        qm_o[:, sl] = q_h.astype(qm_o.dtype)
    if prompt:
        ckv_b = ckv.astype(BF16)
        qab_o[...] = (qa * (A_HEAD_DIM ** -0.5 * LOG2E)).astype(BF16)
        kab_o[...] = ka.astype(BF16)
        kn = _dot(ckv_b, wuk_ref[...])
        for h in range(M_HEADS):
            sl = slice(h * MQ, (h + 1) * MQ)
            km_o[:, sl] = (kn[:, sl] + kpe).astype(BF16)
        vatb_o[0] = vat.astype(BF16)
        vmtb_o[0] = _dot_nt(wuvt_ref[...], ckv_b).astype(BF16)
        _moba_prompt_gating(qa, slopes_ref, ksum_ref, qaug_o, nblk=nblk)
        ksum_ref[pl.ds(pl.program_id(0) % nblk, 1), :] = jnp.sum(ka, axis=0, keepdims=True)
    else:
        qaf_o[...] = qa
        ka_o[...] = ka
        va_o[...] = va
        kpe_o[...] = kpe[:, :M_ROPE]
        qat_o[...] = qa.T


def _const_spec(shape):
    return pl.BlockSpec(shape, lambda *_: (0,) * len(shape))


def _project(x2d, cos_tab, sin_tab, g_attn, w1, g_q, wq, g_kv, extra_weights, *, prompt, tm, seq):
    t = x2d.shape[0]
    assert t % tm == 0 and seq % tm == 0
    n_pos_tiles = seq // tm
    batch = t // seq
    row = lambda i: (i, 0)
    tok_minor = lambda i: (i // n_pos_tiles, 0, i % n_pos_tiles)
    in_specs = [
        pl.BlockSpec((tm, D_MODEL), row),
        pl.BlockSpec((tm, LANES), lambda i: (i % n_pos_tiles, 0)),
        pl.BlockSpec((tm, LANES), lambda i: (i % n_pos_tiles, 0)),
        _const_spec((1, D_MODEL)),
        _const_spec((D_MODEL, N_W1)),
        _const_spec((1, Q_LORA)),
        _const_spec((Q_LORA, 2 * M_HEADS * MQ)),
        _const_spec((1, KV_LORA)),
    ]
    args = [x2d, cos_tab, sin_tab, g_attn, w1, g_q, wq, g_kv]
    f32_out = lambda n: jax.ShapeDtypeStruct((t, n), F32)
    tm_out = lambda n: jax.ShapeDtypeStruct((batch, n, seq), F32)
    out_shape = [tm_out(A_WIDTH), tm_out(A_WIDTH), f32_out(KV_LORA), tm_out(M_ROPE)]
    out_specs = [pl.BlockSpec((None, A_WIDTH, tm), tok_minor), pl.BlockSpec((None, A_WIDTH, tm), tok_minor),
                 pl.BlockSpec((tm, KV_LORA), row), pl.BlockSpec((None, M_ROPE, tm), tok_minor)]
    scratch_shapes = []
    if prompt:
        assert tm == MOBA_BLOCK
        wuk, wuvt, slopes = extra_weights
        nblk = seq // MOBA_BLOCK
        in_specs += [_const_spec((KV_LORA, M_HEADS * MQ)), _const_spec((M_WIDTH, KV_LORA)),
                     pl.BlockSpec(memory_space=pltpu.SMEM)]
        args += [wuk, wuvt, slopes]
        bf_out = lambda n: jax.ShapeDtypeStruct((t, n), BF16)
        vt_shape = jax.ShapeDtypeStruct((batch, nblk, A_WIDTH, MOBA_BLOCK), BF16)
        vt_spec = pl.BlockSpec((None, 1, A_WIDTH, MOBA_BLOCK), lambda i: (i // n_pos_tiles, i % n_pos_tiles, 0, 0))
        out_shape += [bf_out(M_HEADS * MQ), bf_out(A_WIDTH), bf_out(A_WIDTH), vt_shape, bf_out(M_HEADS * MQ),
                      vt_shape, bf_out(A_HEADS * LANES)]
        out_specs += [pl.BlockSpec((tm, M_HEADS * MQ), row), pl.BlockSpec((tm, A_WIDTH), row),
                      pl.BlockSpec((tm, A_WIDTH), row), vt_spec, pl.BlockSpec((tm, M_HEADS * MQ), row), vt_spec,
                      pl.BlockSpec((tm, A_HEADS * LANES), row)]
        scratch_shapes = [pltpu.VMEM((nblk, A_WIDTH), F32)]
    else:
        assert batch == 1
        (wqa,) = extra_weights
        in_specs += [_const_spec((D_MODEL, A_WIDTH))]
        args += [wqa]
        out_shape += [f32_out(M_HEADS * MQ), f32_out(A_WIDTH), f32_out(A_WIDTH), f32_out(A_WIDTH), f32_out(M_ROPE),
                      jax.ShapeDtypeStruct((A_WIDTH, t), F32)]
        out_specs += [pl.BlockSpec((tm, M_HEADS * MQ), row), pl.BlockSpec((tm, A_WIDTH), row),
                      pl.BlockSpec((tm, A_WIDTH), row), pl.BlockSpec((tm, A_WIDTH), row),
                      pl.BlockSpec((tm, M_ROPE), row), pl.BlockSpec((A_WIDTH, tm), lambda i: (0, i))]
    return pl.pallas_call(
        functools.partial(_proj_kernel, prompt=prompt, tm=tm, nblk=seq // MOBA_BLOCK if prompt else None,
                          qk_scale=(M_NOPE + M_ROPE) ** -0.5 * (LOG2E if prompt else 1.0)),
        grid=(t // tm,),
        in_specs=in_specs,
        out_specs=out_specs,
        out_shape=out_shape,
        scratch_shapes=scratch_shapes,
        compiler_params=pltpu.CompilerParams(dimension_semantics=("arbitrary",), vmem_limit_bytes=VMEM_LIMIT),
        name="proj_prompt" if prompt else "proj_sample",
    )(*args)


ATTN_CHUNK = 2
CHUNKS_PER_STEP = 4


ONES_ROWS = 16


def _run_chains(carry, chains, logits_fn, values_fn):
    blk = MOBA_BLOCK
    carry = list(carry)
    ones = jnp.ones((ONES_ROWS, blk), BF16)
    scores = [logits_fn(*chain) for chain in chains]
    for (c, hh, _), logits in zip(chains, scores):
        m, acc = carry[hh]
        m_new = jnp.maximum(m, jnp.max(logits, axis=0, keepdims=True))
        pb = jnp.exp2(logits - m_new).astype(BF16)
        vt_parts = values_fn(c, hh)
        pv = None
        for t, vt in enumerate(vt_parts):
            part = _dot(jnp.concatenate([vt, ones], axis=0), pb[t * blk:(t + 1) * blk])
            pv = part if pv is None else pv + part
        carry[hh] = (m_new, jnp.exp2(m - m_new) * acc + pv)
    return tuple(carry)


def _attend_causal(qi, chunk, dv, logits_fn, values_fn):
    blk = MOBA_BLOCK
    heads = range(2)
    c_diag = qi // chunk
    n_past_steps = c_diag // CHUNKS_PER_STEP

    def step_chains(i, n_chunks, mask_last):
        return [(i * CHUNKS_PER_STEP + u, hh, mask_last and u == n_chunks - 1)
                for u in range(n_chunks) for hh in heads]

    init = tuple((jnp.full((1, blk), NEG, F32), jnp.zeros((dv + ONES_ROWS, blk), F32)) for _ in heads)
    last_step = [functools.partial(_run_chains, chains=step_chains(n_past_steps, n, True),
                                   logits_fn=logits_fn, values_fn=values_fn)
                 for n in range(1, CHUNKS_PER_STEP + 1)]
    carry = lax.switch(c_diag % CHUNKS_PER_STEP, last_step, init)
    carry = lax.fori_loop(
        0, n_past_steps,
        lambda i, cr: _run_chains(cr, step_chains(i, CHUNKS_PER_STEP, False), logits_fn, values_fn), carry)
    return jnp.concatenate([acc[:dv] / acc[dv:dv + 1] for _, acc in carry], axis=0)


def _chunk_key_minus_query(chunk):
    shape = (chunk * MOBA_BLOCK, MOBA_BLOCK)
    return lax.broadcasted_iota(jnp.int32, shape, 0) - lax.broadcasted_iota(jnp.int32, shape, 1)


def _top_blocks(gate, blk_id, n_cand):
    picks = []
    g = gate
    for _ in range(MOBA_TOPK):
        top = jnp.max(g, axis=0, keepdims=True)
        first = jnp.min(jnp.where(g == top, blk_id, n_cand), axis=0, keepdims=True)
        picks.append(first)
        g = jnp.where(blk_id == first, -jnp.inf, g)
    return picks


AUG_SLOPE_ROWS = SUBLANES


def _bf16_split3(x):
    hi = x.astype(BF16).astype(F32)
    mid = (x - hi).astype(BF16).astype(F32)
    lo = (x - hi - mid).astype(BF16).astype(F32)
    return hi, mid, lo


def _moba_key_bias_columns(seq):
    nblk = seq // MOBA_BLOCK
    assert AUG_SLOPE_ROWS + 3 * nblk <= LANES
    pos = jnp.arange(seq)
    col = jnp.arange(LANES)[None, :]
    in_blk = (col >= AUG_SLOPE_ROWS) & (col < AUG_SLOPE_ROWS + 3 * nblk)
    blk_of_col = (col - AUG_SLOPE_ROWS) % nblk
    aug = jnp.where(col < 3, (pos % MOBA_BLOCK)[:, None], 0)
    aug = jnp.where(in_blk & (blk_of_col == (pos // MOBA_BLOCK)[:, None]), 1, aug)
    return aug.astype(BF16)


def _moba_query_bias_rows(attended, slope2, qi, nblk):
    blk = MOBA_BLOCK
    blk_id = lax.broadcasted_iota(jnp.int32, (nblk, blk), 0)
    q_off = lax.broadcasted_iota(jnp.int32, (nblk, blk), 1).astype(F32)
    per_block = jnp.where(attended, 0.0, NEG) + slope2 * ((blk_id - qi) * blk).astype(F32) - slope2 * q_off
    row = lax.broadcasted_iota(jnp.int32, (AUG_SLOPE_ROWS, blk), 0)
    s_hi, s_mid, s_lo = _bf16_split3(jnp.full((AUG_SLOPE_ROWS, blk), slope2, F32))
    slope_rows = jnp.where(row == 0, s_hi, jnp.where(row == 1, s_mid, jnp.where(row == 2, s_lo, 0.0)))
    pad = jnp.zeros((LANES - AUG_SLOPE_ROWS - 3 * nblk, blk), F32)
    return jnp.concatenate([slope_rows, *_bf16_split3(per_block), pad], axis=0)


def _moba_prompt_gating(qa, slopes_ref, ksum_ref, qaug_o, *, nblk):
    blk = MOBA_BLOCK
    step = pl.program_id(0)
    qi = step % nblk

    @pl.when(step == 0)
    def _():
        ksum_ref[...] = jnp.zeros(ksum_ref.shape, F32)

    kmean = ksum_ref[...] * (1.0 / blk)
    blk_id = lax.broadcasted_iota(jnp.int32, (nblk, blk), 0)
    past = blk_id < qi
    lane = lax.broadcasted_iota(jnp.int32, (1, LANES), 1)
    for h in range(A_HEADS):
        pair = slice((h // 2) * LANES, (h // 2 + 1) * LANES)
        head_lanes = (lane >= A_HEAD_DIM) if h % 2 else (lane < A_HEAD_DIM)
        qf_h = jnp.where(head_lanes, qa[:, pair], 0.0)
        gate = _dot_nt(kmean[:, pair], qf_h, precision=lax.Precision.HIGHEST)
        attended = blk_id == qi
        for first in _top_blocks(jnp.where(past, gate, -jnp.inf), blk_id, nblk):
            attended = jnp.logical_or(attended, jnp.logical_and(blk_id == first, past))
        bias_rows = _moba_query_bias_rows(attended, slopes_ref[h] * LOG2E, qi, nblk)
        qaug_o[:, h * LANES:(h + 1) * LANES] = bias_rows.T.astype(BF16)


def _moba_prompt_kernel(qb_ref, qaug_ref, k_ref, kaug_ref, vt_ref, o_ref, *, chunk):
    blk = MOBA_BLOCK
    ck = chunk * blk
    qi = pl.program_id(2)
    lane = lax.broadcasted_iota(jnp.int32, (1, LANES), 1)
    heads = range(2)
    q = []
    for hh in heads:
        head_lanes = (lane >= A_HEAD_DIM) if hh else (lane < A_HEAD_DIM)
        q_h = jnp.where(head_lanes, qb_ref[...], jnp.zeros((), BF16))
        q.append(jnp.concatenate([q_h, qaug_ref[:, hh * LANES:(hh + 1) * LANES]], axis=1))

    rel = _chunk_key_minus_query(chunk)

    def logits(c, hh, masked):
        rows = pl.ds(pl.multiple_of(c * ck, ck), ck)
        keys = jnp.concatenate([k_ref[rows, :], kaug_ref[rows, :]], axis=1)
        s = _dot_nt(keys, q[hh])
        return jnp.where(rel <= (qi - c * chunk) * blk, s, NEG) if masked else s

    def values(c, hh):
        return [vt_ref[c * chunk + t, hh * A_HEAD_DIM:(hh + 1) * A_HEAD_DIM, :] for t in range(chunk)]

    o_ref[...] = _attend_causal(qi, chunk, A_HEAD_DIM, logits, values).T


def _moba_prompt(qab, qaug, kab, vatb, *, batch, seq):
    nblk = seq // MOBA_BLOCK
    blk = MOBA_BLOCK
    assert nblk % (ATTN_CHUNK * CHUNKS_PER_STEP) == 0
    q_spec = pl.BlockSpec((blk, LANES), lambda b, hp, qi: (b * nblk + qi, hp))
    return pl.pallas_call(
        functools.partial(_moba_prompt_kernel, chunk=ATTN_CHUNK),
        grid=(batch, HEAD_PAIRS, nblk),
        in_specs=[
            q_spec,
            pl.BlockSpec((blk, 2 * LANES), lambda b, hp, qi: (b * nblk + qi, hp)),
            pl.BlockSpec((seq, LANES), lambda b, hp, qi: (b, hp)),
            pl.BlockSpec((seq, LANES), lambda b, hp, qi: (0, 0)),
            pl.BlockSpec((None, nblk, LANES, blk), lambda b, hp, qi: (b, 0, hp, 0)),
        ],
        out_specs=q_spec,
        out_shape=jax.ShapeDtypeStruct((batch * seq, A_WIDTH), F32),
        compiler_params=pltpu.CompilerParams(dimension_semantics=("arbitrary",) * 3, vmem_limit_bytes=VMEM_LIMIT),
        name="moba_prompt",
    )(qab, qaug, kab, _moba_key_bias_columns(seq), vatb)


GATE_UNITS_PER_STEP = 2


def _mla_prompt_kernel(pt_ref, *refs, chunk, nblk, gate_steps, gate_nblk):
    n_gate_pages = GATE_UNITS_PER_STEP * PAGES_PER_STEP
    pages = refs[:n_gate_pages]
    (qat_ref, qs_ref, knew_ref, q_ref, k_ref, vt_ref, o_ref, sel_ref, score_ref, qb_ref,
     part_ref) = refs[n_gate_pages:]
    step = (pl.program_id(0) * HEAD_PAIRS + pl.program_id(1)) * nblk + pl.program_id(2)
    steps_per_seq = gate_steps // GATE_UNITS_PER_STEP
    for u in range(GATE_UNITS_PER_STEP):
        _moba_gate_unit(step // steps_per_seq, (step % steps_per_seq) * GATE_UNITS_PER_STEP + u,
                        pages[u * PAGES_PER_STEP:(u + 1) * PAGES_PER_STEP], u * PAGES_PER_STEP,
                        qat_ref, qs_ref, knew_ref, sel_ref, score_ref, qb_ref, part_ref,
                        n_steps=gate_steps, nblk=gate_nblk)

    blk = MOBA_BLOCK
    ck = chunk * blk
    qi = pl.program_id(2)
    heads = range(2)
    q = [q_ref[:, hh * MQ:(hh + 1) * MQ] for hh in heads]

    rel = _chunk_key_minus_query(chunk)

    def logits(c, hh, masked):
        rows = pl.ds(pl.multiple_of(c * ck, ck), ck)
        s = _dot_nt(k_ref[rows, hh * MQ:(hh + 1) * MQ], q[hh])
        return jnp.where(rel <= (qi - c * chunk) * blk, s, NEG) if masked else s

    def values(c, hh):
        return [vt_ref[c * chunk + t, hh * M_VDIM:(hh + 1) * M_VDIM, :] for t in range(chunk)]

    o_ref[...] = _attend_causal(qi, chunk, M_VDIM, logits, values).T


def _mla_prompt_and_gate(qm, km, vmtb, pt_flat, cache_kt, qat, q3, knew3, *, batch, seq, dec_batch, n_pages):
    nblk = seq // MOBA_BLOCK
    blk = MOBA_BLOCK
    gate_nblk = n_pages // PAGES_PER_BLOCK
    gate_steps = n_pages // PAGES_PER_STEP
    pages_per_grid_step = GATE_UNITS_PER_STEP * PAGES_PER_STEP
    steps_per_seq = gate_steps // GATE_UNITS_PER_STEP
    assert nblk % (ATTN_CHUNK * CHUNKS_PER_STEP) == 0
    assert n_pages % pages_per_grid_step == 0 and gate_nblk >= MOBA_TOPK and dec_batch <= LANES
    assert batch * HEAD_PAIRS * nblk == dec_batch * steps_per_seq

    def step_of(b, hp, qi):
        return (b * HEAD_PAIRS + hp) * nblk + qi

    def page_spec(i):
        def index_map(b, hp, qi, pt):
            t = step_of(b, hp, qi)
            page = (t % steps_per_seq) * pages_per_grid_step + i
            return (pt[page * dec_batch + t // steps_per_seq], 0, 0)
        return pl.BlockSpec((None, A_WIDTH, PAGE_SIZE), index_map)

    seq_vec = pl.BlockSpec((None, 1, A_WIDTH), lambda b, hp, qi, pt: (step_of(b, hp, qi) // steps_per_seq, 0, 0))
    grid_spec = pltpu.PrefetchScalarGridSpec(
        num_scalar_prefetch=1,
        grid=(batch, HEAD_PAIRS, nblk),
        in_specs=([page_spec(i) for i in range(pages_per_grid_step)] + [
            pl.BlockSpec((A_WIDTH, dec_batch), lambda b, hp, qi, pt: (0, 0)),
            seq_vec,
            seq_vec,
            pl.BlockSpec((blk, 2 * MQ), lambda b, hp, qi, pt: (b * nblk + qi, hp)),
            pl.BlockSpec((seq, 2 * MQ), lambda b, hp, qi, pt: (b, hp)),
            pl.BlockSpec((None, nblk, LANES, blk), lambda b, hp, qi, pt: (b, 0, hp, 0)),
        ]),
        out_specs=[
            pl.BlockSpec((blk, LANES), lambda b, hp, qi, pt: (b * nblk + qi, hp)),
            pl.BlockSpec((None, SUBLANES, LANES), lambda b, hp, qi, pt: (step_of(b, hp, qi) // steps_per_seq, 0, 0)),
            pl.BlockSpec((None, pages_per_grid_step, A_HEADS, PAGE_SIZE),
                         lambda b, hp, qi, pt: (step_of(b, hp, qi) // steps_per_seq,
                                                step_of(b, hp, qi) % steps_per_seq, 0, 0)),
        ],
        scratch_shapes=[pltpu.VMEM((A_WIDTH, LANES), F32), pltpu.VMEM((gate_nblk * A_HEADS, LANES), F32)],
    )
    return pl.pallas_call(
        functools.partial(_mla_prompt_kernel, chunk=ATTN_CHUNK, nblk=nblk, gate_steps=gate_steps,
                          gate_nblk=gate_nblk),
        grid_spec=grid_spec,
        out_shape=[jax.ShapeDtypeStruct((batch * seq, M_WIDTH), F32),
                   jax.ShapeDtypeStruct((dec_batch, SUBLANES, LANES), jnp.int32),
                   jax.ShapeDtypeStruct((dec_batch, n_pages, A_HEADS, PAGE_SIZE), F32)],
        compiler_params=pltpu.CompilerParams(dimension_semantics=("arbitrary",) * 3, vmem_limit_bytes=VMEM_LIMIT),
        name="mla_prompt_moba_gate",
    )(pt_flat, *([cache_kt] * pages_per_grid_step), qat, q3, knew3, qm, km, vmtb)


def _finish_kernel(x_ref, oa_ref, om_ref, goa_ref, gom_ref, wo_ref, gf_ref, wgu_ref, wd_ref, gfin_ref, y_ref):
    oa = _rms(oa_ref[...], goa_ref[...])
    om = _rms(om_ref[...], gom_ref[...])
    mix = jnp.concatenate([oa, om], axis=-1).astype(BF16)
    h = x_ref[...] + _dot(mix, wo_ref[...])
    hn = _rms(h, gf_ref[...]).astype(BF16)
    gu = _dot(hn, wgu_ref[...])
    g = gu[:, :D_FF]
    u = gu[:, D_FF:]
    act = (g / (1.0 + jnp.exp(-g)) * u).astype(BF16)
    h = h + _dot(act, wd_ref[...])
    y_ref[...] = _rms(h, gfin_ref[...])


def _finish(x2d, oa, om, g_oa, g_om, wo, g_ffn, wgu, wd, g_final, *, tm, name):
    t = x2d.shape[0]
    assert t % tm == 0
    row = lambda i: (i, 0)
    single = pl.Buffered(1)
    wspec = lambda shape: pl.BlockSpec(shape, lambda i: (0, 0), pipeline_mode=single)
    return pl.pallas_call(
        _finish_kernel,
        grid=(t // tm,),
        in_specs=[
            pl.BlockSpec((tm, D_MODEL), row),
            pl.BlockSpec((tm, A_WIDTH), row),
            pl.BlockSpec((tm, M_WIDTH), row),
            _const_spec((1, A_WIDTH)),
            _const_spec((1, M_WIDTH)),
            wspec((A_WIDTH + M_WIDTH, D_MODEL)),
            _const_spec((1, D_MODEL)),
            wspec((D_MODEL, 2 * D_FF)),
            wspec((D_FF, D_MODEL)),
            _const_spec((1, D_MODEL)),
        ],
        out_specs=pl.BlockSpec((tm, D_MODEL), row),
        out_shape=jax.ShapeDtypeStruct((t, D_MODEL), F32),
        compiler_params=pltpu.CompilerParams(dimension_semantics=("arbitrary",), vmem_limit_bytes=VMEM_LIMIT),
        name=name,
    )(x2d, oa, om, g_oa, g_om, wo, g_ffn, wgu, wd, g_final)


PAGES_PER_STEP = 16
PAGES_PER_BLOCK = MOBA_BLOCK // PAGE_SIZE
BLOCKS_PER_STEP = PAGES_PER_STEP // PAGES_PER_BLOCK


def _moba_gate_unit(b, s, pages, score_page0, qat_ref, q_ref, knew_ref, sel_ref, score_ref, qb_ref, part_ref, *,
                    n_steps, nblk):
    lane = lax.broadcasted_iota(jnp.int32, (1, LANES), 1)

    @pl.when(s == 0)
    def _():
        seq_id = lax.broadcasted_iota(jnp.int32, (1, qat_ref.shape[1]), 1)
        col = jnp.sum(jnp.where(seq_id == b, qat_ref[...], 0.0), axis=1, keepdims=True)
        qb_ref[...] = jnp.broadcast_to(col, qb_ref.shape)

    qb = qb_ref[...]
    for i in range(BLOCKS_PER_STEP):
        tiles = []
        for pg in range(PAGES_PER_BLOCK):
            prod = pages[PAGES_PER_BLOCK * i + pg][...] * qb
            tiles.append(jnp.concatenate(
                [jnp.sum(prod[h * A_HEAD_DIM:(h + 1) * A_HEAD_DIM], axis=0, keepdims=True)
                 for h in range(A_HEADS)], axis=0))
            score_ref[score_page0 + PAGES_PER_BLOCK * i + pg] = tiles[pg]
        part_ref[pl.ds(pl.multiple_of((s * BLOCKS_PER_STEP + i) * A_HEADS, A_HEADS), A_HEADS), :] = (
            functools.reduce(jnp.add, tiles))

    @pl.when(s == n_steps - 1)
    def _():
        gate = jnp.sum(part_ref[...], axis=1, keepdims=True) * (1.0 / MOBA_BLOCK)
        prod_new = knew_ref[...] * q_ref[...]
        lane_head = lax.broadcasted_iota(jnp.int32, (1, A_WIDTH), 1) // A_HEAD_DIM
        head_row = lax.broadcasted_iota(jnp.int32, (A_HEADS, 1), 0)
        g_new = jnp.zeros((A_HEADS, 1), F32)
        for h in range(A_HEADS):
            g_h = jnp.sum(jnp.where(lane_head == h, prod_new, 0.0), axis=1, keepdims=True)
            g_new = jnp.where(head_row == h, g_h * (1.0 / MOBA_BLOCK), g_new)
        n_cand = nblk + 1
        cand = [gate[j * A_HEADS:(j + 1) * A_HEADS] for j in range(nblk)] + [g_new]
        cand = [g if j < nblk else jnp.full_like(g, -jnp.inf) for j, g in enumerate(cand)]
        out = jnp.zeros((A_HEADS, LANES), jnp.int32)
        for r in range(MOBA_TOPK):
            top = functools.reduce(jnp.maximum, cand)
            first = functools.reduce(jnp.minimum, [jnp.where(g == top, j, n_cand) for j, g in enumerate(cand)])
            out = jnp.where(lane == r, first, out)
            cand = [jnp.where(first == j, -jnp.inf, g) for j, g in enumerate(cand)]
        sel_ref[...] = out


SLABS = 2 * MOBA_TOPK * PAGES_PER_BLOCK


def _moba_sample_kernel(pt_ref, sel_ref, slopes_ref, *refs, past_len):
    v_refs = refs[:SLABS]
    score_ref, q_ref, knew_ref, vnew_ref, o_ref = refs[SLABS:]
    b = pl.program_id(0)
    hp = pl.program_id(1)
    n_keys = MOBA_TOPK * MOBA_BLOCK
    slabs_per_head = MOBA_TOPK * PAGES_PER_BLOCK
    lane = lax.broadcasted_iota(jnp.int32, (1, n_keys), 1)
    scale = A_HEAD_DIM ** -0.5
    outs = []
    for hh in range(2):
        h = 2 * hp + hh
        cols = slice(hh * A_HEAD_DIM, (hh + 1) * A_HEAD_DIM)
        kpos = lane % MOBA_BLOCK
        pieces = []
        for r in range(MOBA_TOPK):
            blk = sel_ref[(b * A_HEADS + h) * MOBA_TOPK + r]
            kpos = kpos + jnp.where(lane // MOBA_BLOCK == r, blk * MOBA_BLOCK, 0)
            pieces += [score_ref[blk * PAGES_PER_BLOCK + pg, pl.ds(h, 1), :] for pg in range(PAGES_PER_BLOCK)]
        logits = jnp.concatenate(pieces, axis=1) * scale - slopes_ref[h] * (past_len - kpos).astype(F32)
        s_new = jnp.sum(knew_ref[:, cols] * q_ref[:, cols], axis=1, keepdims=True) * scale
        m = jnp.maximum(s_new, jnp.max(logits, axis=1, keepdims=True))
        p_new = jnp.exp(s_new - m)
        p = jnp.exp(logits - m)
        l = p_new + jnp.sum(p, axis=1, keepdims=True)
        vt = jnp.concatenate([v_refs[hh * slabs_per_head + i][...] for i in range(slabs_per_head)], axis=1)
        pv = _dot_nt(jnp.broadcast_to(p, (SUBLANES, n_keys)).astype(BF16), vt.astype(BF16))[:1]
        outs.append((p_new * vnew_ref[:, cols] + pv) / l)
    o_ref[...] = jnp.concatenate(outs, axis=1)


def _moba_sample(pt_flat, sel_flat, slopes, cache_vt, scores, q3, knew3, vnew3, *, dec_batch, n_pages):
    def slab_spec(slab):
        hh, rem = divmod(slab, MOBA_TOPK * PAGES_PER_BLOCK)
        r, pg = divmod(rem, PAGES_PER_BLOCK)

        def index_map(b, hp, pt, sel):
            blk = sel[(b * A_HEADS + 2 * hp + hh) * MOBA_TOPK + r]
            return (pt[(blk * PAGES_PER_BLOCK + pg) * dec_batch + b], 2 * hp + hh, 0, 0)

        return pl.BlockSpec((None, None, A_HEAD_DIM, PAGE_SIZE), index_map)

    vec_spec = pl.BlockSpec((None, 1, LANES), lambda b, hp, pt, sel: (b, 0, hp))
    slabs = [slab_spec(i) for i in range(SLABS)]
    grid_spec = pltpu.PrefetchScalarGridSpec(
        num_scalar_prefetch=2,
        grid=(dec_batch, HEAD_PAIRS),
        in_specs=([pl.BlockSpec(memory_space=pltpu.SMEM)] + slabs
                  + [pl.BlockSpec((None, n_pages, A_HEADS, PAGE_SIZE), lambda b, hp, pt, sel: (b, 0, 0, 0)),
                     vec_spec, vec_spec, vec_spec]),
        out_specs=vec_spec,
    )
    return pl.pallas_call(
        functools.partial(_moba_sample_kernel, past_len=n_pages * PAGE_SIZE),
        grid_spec=grid_spec,
        out_shape=jax.ShapeDtypeStruct((dec_batch, 1, A_WIDTH), F32),
        compiler_params=pltpu.CompilerParams(dimension_semantics=("arbitrary", "arbitrary"),
                                             vmem_limit_bytes=VMEM_LIMIT),
        name="moba_sample_attn",
    )(pt_flat, sel_flat, slopes, *([cache_vt] * SLABS), scores, q3, knew3, vnew3)


MLA_GROUPS = 4
MLA_GROUP_PAGES = 8
MLA_PAGES_PER_STEP = MLA_GROUPS * MLA_GROUP_PAGES


def _mla_sample_kernel(pt_ref, *refs, n_steps):
    ckv_pages = refs[:MLA_PAGES_PER_STEP]
    kpe_pages = refs[MLA_PAGES_PER_STEP:2 * MLA_PAGES_PER_STEP]
    (q_ref, ckvn_ref, kpen_ref, wuk_ref, wuv_ref, o_ref, qlat_ref, m_ref, l_ref,
     acc_ref) = refs[2 * MLA_PAGES_PER_STEP:]
    s = pl.program_id(1)
    q8 = q_ref[...]
    q_pe = q8[:, :M_ROPE]

    @pl.when(s == 0)
    def _():
        head_of_lane = lax.broadcasted_iota(jnp.int32, (M_HEADS, M_HEADS * MQ), 1) // MQ
        head_of_row = lax.broadcasted_iota(jnp.int32, (M_HEADS, M_HEADS * MQ), 0)
        q_bd = jnp.where(head_of_lane == head_of_row, jnp.concatenate([q8] * M_HEADS, axis=1), 0.0)
        qlat_ref[...] = _dot_nt(q_bd.astype(BF16), wuk_ref[...])
        m_ref[...] = jnp.full(m_ref.shape, NEG, F32)
        l_ref[...] = jnp.zeros(l_ref.shape, F32)
        acc_ref[...] = jnp.zeros(acc_ref.shape, F32)

    qlat = qlat_ref[...]
    qlat_b = qlat.astype(BF16)
    qpe_b = q_pe.astype(BF16)
    pcs, scores = [], []
    for g in range(MLA_GROUPS):
        pages = range(g * MLA_GROUP_PAGES, (g + 1) * MLA_GROUP_PAGES)
        pcs.append(jnp.concatenate([ckv_pages[i][...].astype(BF16) for i in pages], axis=0))
        kpt = jnp.concatenate([kpe_pages[i][...].astype(BF16) for i in pages], axis=1)
        scores.append(_dot_nt(qlat_b, pcs[g]) + _dot(qpe_b, kpt))
    for g in range(MLA_GROUPS):
        m_old = m_ref[g]
        m_new = jnp.maximum(m_old, jnp.max(scores[g], axis=1, keepdims=True))
        alpha = jnp.exp(m_old - m_new)
        p = jnp.exp(scores[g] - m_new)
        m_ref[g] = m_new
        l_ref[g] = alpha * l_ref[g] + jnp.sum(p, axis=1, keepdims=True)
        acc_ref[g] = alpha * acc_ref[g] + _dot(p.astype(BF16), pcs[g])

    @pl.when(s == n_steps - 1)
    def _():
        ckvn = ckvn_ref[...]
        s_new = (jnp.sum(qlat * ckvn, axis=1, keepdims=True)
                 + jnp.sum(q_pe * kpen_ref[...], axis=1, keepdims=True))
        m_fin = s_new
        for g in range(MLA_GROUPS):
            m_fin = jnp.maximum(m_fin, m_ref[g])
        p_new = jnp.exp(s_new - m_fin)
        num = p_new * ckvn
        den = p_new
        for g in range(MLA_GROUPS):
            a = jnp.exp(m_ref[g] - m_fin)
            num = num + a * acc_ref[g]
            den = den + a * l_ref[g]
        o_lat = num / den
        res = _dot(o_lat.astype(BF16), wuv_ref[...])
        head_of_lane = lax.broadcasted_iota(jnp.int32, (M_HEADS, M_WIDTH), 1) // M_VDIM
        head_of_row = lax.broadcasted_iota(jnp.int32, (M_HEADS, M_WIDTH), 0)
        o_ref[...] = jnp.sum(jnp.where(head_of_lane == head_of_row, res, 0.0), axis=0, keepdims=True)


def _mla_sample(pt_flat, cache_ckv, cache_kpet, q8, ckvn3, kpen3, wuk, wuv, *, dec_batch, n_pages):
    assert n_pages % MLA_PAGES_PER_STEP == 0
    n_steps = n_pages // MLA_PAGES_PER_STEP

    def page_spec(i, shape):
        return pl.BlockSpec((None,) + shape,
                            lambda b, s, pt: (pt[(s * MLA_PAGES_PER_STEP + i) * dec_batch + b], 0, 0))

    per_b = lambda shape: pl.BlockSpec((None,) + shape, lambda b, s, pt: (b, 0, 0))
    grid_spec = pltpu.PrefetchScalarGridSpec(
        num_scalar_prefetch=1,
        grid=(dec_batch, n_steps),
        in_specs=([page_spec(i, (PAGE_SIZE, KV_LORA)) for i in range(MLA_PAGES_PER_STEP)]
                  + [page_spec(i, (M_ROPE, PAGE_SIZE)) for i in range(MLA_PAGES_PER_STEP)]
                  + [per_b((M_HEADS, MQ)), per_b((1, KV_LORA)), per_b((1, M_ROPE)),
                     pl.BlockSpec((KV_LORA, M_HEADS * MQ), lambda b, s, pt: (0, 0)),
                     pl.BlockSpec((KV_LORA, M_WIDTH), lambda b, s, pt: (0, 0))]),
        out_specs=per_b((1, M_WIDTH)),
        scratch_shapes=[pltpu.VMEM((M_HEADS, KV_LORA), F32), pltpu.VMEM((MLA_GROUPS, M_HEADS, 1), F32),
                        pltpu.VMEM((MLA_GROUPS, M_HEADS, 1), F32), pltpu.VMEM((MLA_GROUPS, M_HEADS, KV_LORA), F32)],
    )
    return pl.pallas_call(
        functools.partial(_mla_sample_kernel, n_steps=n_steps),
        grid_spec=grid_spec,
        out_shape=jax.ShapeDtypeStruct((dec_batch, 1, M_WIDTH), F32),
        compiler_params=pltpu.CompilerParams(dimension_semantics=("arbitrary", "arbitrary"),
                                             vmem_limit_bytes=VMEM_LIMIT),
        name="mla_sample",
    )(pt_flat, *([cache_ckv] * MLA_PAGES_PER_STEP), *([cache_kpet] * MLA_PAGES_PER_STEP), q8, ckvn3, kpen3, wuk,
      wuv)


def _rot_half_cols(w):
    half = M_ROPE // 2
    return jnp.concatenate([-w[..., half:], w[..., :half]], axis=-1)


def _prep_weights(w_in, w_q_up, w_kv_up):
    kr = w_in[:, C_KR:C_KR + M_ROPE]
    pad = jnp.zeros((D_MODEL, LANES - M_ROPE), F32)
    w1 = jnp.concatenate([w_in[:, :C_KR], kr, pad, _rot_half_cols(kr), pad], axis=1).astype(BF16)
    wqa = w_in[:, C_QA:C_QA + A_WIDTH]
    wq3 = w_q_up.reshape(Q_LORA, M_HEADS, M_NOPE + M_ROPE)
    nope, rope = wq3[..., :M_NOPE], wq3[..., M_NOPE:]
    z = lambda n: jnp.zeros((Q_LORA, M_HEADS, n), F32)
    wq_a = jnp.concatenate([rope, nope, z(MQ - M_NOPE - M_ROPE)], axis=-1).reshape(Q_LORA, M_HEADS * MQ)
    wq_b = jnp.concatenate([_rot_half_cols(rope), z(MQ - M_ROPE)], axis=-1).reshape(Q_LORA, M_HEADS * MQ)
    wq = jnp.concatenate([wq_a, wq_b], axis=1).astype(BF16)
    wkv3 = w_kv_up.reshape(KV_LORA, M_HEADS, M_NOPE + M_VDIM)
    w_uk, w_uv = wkv3[..., :M_NOPE], wkv3[..., M_NOPE:]
    zk = lambda n: jnp.zeros((KV_LORA, M_HEADS, n), F32)
    wuk = jnp.concatenate([zk(M_ROPE), w_uk, zk(MQ - M_ROPE - M_NOPE)], axis=-1).reshape(KV_LORA, M_HEADS * MQ)
    wuv = w_uv.reshape(KV_LORA, M_WIDTH)
    return w1, wqa, wq, wuk.astype(BF16), wuv.astype(BF16), wuv.T.astype(BF16)


def _rope_tables(pos):
    half = M_ROPE // 2
    inv = ROPE_THETA ** (-jnp.arange(half, dtype=F32) / half)
    ang = pos.astype(F32)[:, None] * inv[None, :]
    n = pos.shape[0]
    cos = jnp.concatenate([jnp.cos(ang), jnp.cos(ang), jnp.ones((n, M_NOPE), F32),
                           jnp.zeros((n, MQ - M_ROPE - M_NOPE), F32)], axis=1)
    sin = jnp.concatenate([jnp.sin(ang), jnp.sin(ang), jnp.zeros((n, MQ - M_ROPE), F32)], axis=1)
    return cos, sin


def kernel(x_prompt, x_sample, cache_moba_k, cache_moba_v, cache_mla_ckv, cache_mla_kpe, page_table, g_attn, w_in,
           g_q, w_q_up, g_kv, w_kv_up, g_out_moba, g_out_mla, w_o, g_ffn, w_gate_up, w_down, g_final):
    batch, seq, _ = x_prompt.shape
    dec_batch, dec_seq, _ = x_sample.shape
    n_pool = cache_moba_k.shape[1]
    n_pages = page_table.shape[1]
    past_len = n_pages * PAGE_SIZE
    assert w_in.shape[0] == 1 and dec_seq == 1 and seq % MOBA_BLOCK == 0 and past_len % MOBA_BLOCK == 0

    w1, wqa, wq, wuk, wuv, wuvt = _prep_weights(w_in[0], w_q_up[0], w_kv_up[0])
    wo = w_o[0].astype(BF16)
    wgu = w_gate_up[0].astype(BF16)
    wd = w_down[0].astype(BF16)
    row2 = lambda a: a.reshape(1, -1)
    slopes = 2.0 ** (-8.0 * jnp.arange(1, A_HEADS + 1, dtype=F32) / A_HEADS)

    cos_p, sin_p = _rope_tables(jnp.arange(seq))
    xp = x_prompt.reshape(batch * seq, D_MODEL)
    (kat, vat, ckv, kpet, qm, qab, kab, vatb, km, vmtb, qaug) = _project(
        xp, cos_p, sin_p, row2(g_attn[0]), w1, row2(g_q[0]), wq, row2(g_kv[0]), (wuk, wuvt, slopes),
        prompt=True, tm=MOBA_BLOCK, seq=seq)
    cos_s, sin_s = _rope_tables(jnp.full((dec_batch,), past_len, jnp.int32))
    xs = x_sample.reshape(dec_batch, D_MODEL)
    (kat_s, vat_s, ckv_s, kpet_s, qm_s, qa_s, ka_s, va_s, kpe_s, qat_s) = _project(
        xs, cos_s, sin_s, row2(g_attn[0]), w1, row2(g_q[0]), wq, row2(g_kv[0]), (wqa,),
        prompt=False, tm=dec_batch, seq=dec_batch)
    pt_flat = page_table.T.reshape(-1)
    ckt = jnp.transpose(cache_moba_k[0], (0, 2, 3, 1))
    cvt = jnp.transpose(cache_moba_v[0], (0, 2, 3, 1))
    kpet_cache = jnp.transpose(cache_mla_kpe[0], (0, 2, 1))
    q3 = qa_s.reshape(dec_batch, 1, A_WIDTH)
    knew3 = ka_s.reshape(dec_batch, 1, A_WIDTH)
    vnew3 = va_s.reshape(dec_batch, 1, A_WIDTH)

    oa = _moba_prompt(qab, qaug, kab, vatb, batch=batch, seq=seq)
    om, sel, scores_s = _mla_prompt_and_gate(
        qm, km, vmtb, pt_flat, ckt.reshape(n_pool, A_WIDTH, PAGE_SIZE), qat_s, q3, knew3,
        batch=batch, seq=seq, dec_batch=dec_batch, n_pages=n_pages)
    finish_w = (row2(g_out_moba[0]), row2(g_out_mla[0]), wo, row2(g_ffn[0]), wgu, wd, row2(g_final))
    y_prompt = _finish(xp, oa, om, *finish_w, tm=512, name="finish_prompt").reshape(batch, seq, D_MODEL)

    oa_s = _moba_sample(pt_flat, sel[:, :A_HEADS, :MOBA_TOPK].reshape(-1), slopes, cvt, scores_s, q3, knew3, vnew3,
                        dec_batch=dec_batch, n_pages=n_pages)
    om_s = _mla_sample(pt_flat, cache_mla_ckv[0], kpet_cache, qm_s.reshape(dec_batch, M_HEADS, MQ),
                       ckv_s.reshape(dec_batch, 1, KV_LORA), kpe_s.reshape(dec_batch, 1, M_ROPE), wuk, wuv,
                       dec_batch=dec_batch, n_pages=n_pages)
    y_sample = _finish(xs, oa_s.reshape(dec_batch, A_WIDTH), om_s.reshape(dec_batch, M_WIDTH), *finish_w,
                       tm=dec_batch, name="finish_sample").reshape(dec_batch, 1, D_MODEL)

    def heads_last(t_minor, n_tok_batch, n_tok):
        return jnp.transpose(t_minor.reshape(1, n_tok_batch, A_HEADS, A_HEAD_DIM, n_tok), (0, 1, 4, 2, 3))

    k_sample = jnp.transpose(kat_s.reshape(1, 1, A_HEADS, A_HEAD_DIM, dec_batch), (0, 4, 1, 2, 3))
    v_sample = jnp.transpose(vat_s.reshape(1, 1, A_HEADS, A_HEAD_DIM, dec_batch), (0, 4, 1, 2, 3))
    return (y_prompt, y_sample,
            heads_last(kat, batch, seq), heads_last(vat, batch, seq),
            ckv.reshape(1, batch, seq, KV_LORA), jnp.transpose(kpet.reshape(1, batch, M_ROPE, seq), (0, 1, 3, 2)),
            k_sample, v_sample,
            ckv_s.reshape(1, dec_batch, 1, KV_LORA),
            jnp.transpose(kpet_s.reshape(1, 1, M_ROPE, dec_batch), (0, 3, 1, 2)))
```

```python
import functools

import jax
import jax.numpy as jnp
from jax import lax
from jax.experimental import pallas as pl
from jax.experimental.pallas import tpu as pltpu

F32 = jnp.float32
BF16 = jnp.bfloat16

D_MODEL = 1024
A_HEADS = 8
A_HEAD_DIM = 64
A_WIDTH = A_HEADS * A_HEAD_DIM
MOBA_BLOCK = 256
MOBA_TOPK = 3
M_HEADS = 8
M_NOPE = 64
M_ROPE = 32
M_VDIM = 64
M_WIDTH = M_HEADS * M_VDIM
Q_LORA = 256
KV_LORA = 128
ROPE_THETA = 10000.0
D_FF = 2816
PAGE_SIZE = 128
EPS = 1e-6

LANES = 128
SUBLANES = 8
HEAD_PAIRS = A_HEADS // 2
MQ = 128
NEG = -1e30
LOG2E = 1.4426950408889634
VMEM_LIMIT = 56 * 1024 * 1024

C_QA, C_KA, C_VA = 0, A_WIDTH, 2 * A_WIDTH
C_QD = 3 * A_WIDTH
C_KVD = C_QD + Q_LORA
C_KR = C_KVD + KV_LORA
C_KRR = C_KR + LANES
N_W1 = C_KRR + LANES


def _rms(x, g):
    return x * lax.rsqrt(jnp.mean(x * x, axis=-1, keepdims=True) + EPS) * g


def _dot(a, b, precision=None):
    return jnp.dot(a, b, preferred_element_type=F32, precision=precision)


def _dot_nt(a, b, precision=None):
    return lax.dot_general(a, b, (((1,), (1,)), ((), ())), preferred_element_type=F32, precision=precision)


def _proj_kernel(x_ref, cos_ref, sin_ref, ga_ref, w1_ref, gq_ref, wq_ref, gkv_ref, *rest, prompt, tm, qk_scale,
                 nblk):
    if prompt:
        (wuk_ref, wuvt_ref, slopes_ref, kat_o, vat_o, ckv_o, kpet_o, qm_o, qab_o, kab_o, vatb_o, km_o, vmtb_o,
         qaug_o, ksum_ref) = rest
    else:
        wqa_ref, kat_o, vat_o, ckv_o, kpet_o, qm_o, qaf_o, ka_o, va_o, kpe_o, qat_o = rest
    xn = _rms(x_ref[...], ga_ref[...])
    xb = xn.astype(BF16)
    z = _dot(xb, w1_ref[...])
    ka = z[:, C_KA:C_KA + A_WIDTH]
    va = z[:, C_VA:C_VA + A_WIDTH]
    if prompt:
        qa = z[:, C_QA:C_QA + A_WIDTH]
    else:
        qa = _dot(xn, wqa_ref[...], precision=lax.Precision.HIGHEST)
    cos = cos_ref[...]
    sin = sin_ref[...]
    kpe = z[:, C_KR:C_KR + LANES] * cos + z[:, C_KRR:C_KRR + LANES] * sin
    ckv = _rms(z[:, C_KVD:C_KVD + KV_LORA], gkv_ref[...])
    qn = _rms(z[:, C_QD:C_QD + Q_LORA], gq_ref[...]).astype(BF16)
    qab = _dot(qn, wq_ref[...])
    kat = ka.T
    vat = va.T
    kat_o[...] = kat
    vat_o[...] = vat
    ckv_o[...] = ckv
    kpet_o[...] = kpe.T[:M_ROPE]
    for h in range(M_HEADS):
        sl = slice(h * MQ, (h + 1) * MQ)
        q_h = (qab[:, sl] * cos + qab[:, M_HEADS * MQ + h * MQ:M_HEADS * MQ + (h + 1) * MQ] * sin) * qk_scale
        qm_o[:, sl] = q_h.astype(qm_o.dtype)
    if prompt:
        ckv_b = ckv.astype(BF16)
        qab_o[...] = (qa * (A_HEAD_DIM ** -0.5 * LOG2E)).astype(BF16)
        kab_o[...] = ka.astype(BF16)
        kn = _dot(ckv_b, wuk_ref[...])
        for h in range(M_HEADS):
            sl = slice(h * MQ, (h + 1) * MQ)
            km_o[:, sl] = (kn[:, sl] + kpe).astype(BF16)
        vatb_o[0] = vat.astype(BF16)
        vmtb_o[0] = _dot_nt(wuvt_ref[...], ckv_b).astype(BF16)
        _moba_prompt_gating(qa, slopes_ref, ksum_ref, qaug_o, nblk=nblk)
        ksum_ref[pl.ds(pl.program_id(0) % nblk, 1), :] = jnp.sum(ka, axis=0, keepdims=True)
    else:
        qaf_o[...] = qa
        ka_o[...] = ka
        va_o[...] = va
        kpe_o[...] = kpe[:, :M_ROPE]
        qat_o[...] = qa.T


def _const_spec(shape):
    return pl.BlockSpec(shape, lambda *_: (0,) * len(shape))


def _project(x2d, cos_tab, sin_tab, g_attn, w1, g_q, wq, g_kv, extra_weights, *, prompt, tm, seq):
    t = x2d.shape[0]
    assert t % tm == 0 and seq % tm == 0
    n_pos_tiles = seq // tm
    batch = t // seq
    row = lambda i: (i, 0)
    tok_minor = lambda i: (i // n_pos_tiles, 0, i % n_pos_tiles)
    in_specs = [
        pl.BlockSpec((tm, D_MODEL), row),
        pl.BlockSpec((tm, LANES), lambda i: (i % n_pos_tiles, 0)),
        pl.BlockSpec((tm, LANES), lambda i: (i % n_pos_tiles, 0)),
        _const_spec((1, D_MODEL)),
        _const_spec((D_MODEL, N_W1)),
        _const_spec((1, Q_LORA)),
        _const_spec((Q_LORA, 2 * M_HEADS * MQ)),
        _const_spec((1, KV_LORA)),
    ]
    args = [x2d, cos_tab, sin_tab, g_attn, w1, g_q, wq, g_kv]
    f32_out = lambda n: jax.ShapeDtypeStruct((t, n), F32)
    tm_out = lambda n: jax.ShapeDtypeStruct((batch, n, seq), F32)
    out_shape = [tm_out(A_WIDTH), tm_out(A_WIDTH), f32_out(KV_LORA), tm_out(M_ROPE)]
    out_specs = [pl.BlockSpec((None, A_WIDTH, tm), tok_minor), pl.BlockSpec((None, A_WIDTH, tm), tok_minor),
                 pl.BlockSpec((tm, KV_LORA), row), pl.BlockSpec((None, M_ROPE, tm), tok_minor)]
    scratch_shapes = []
    if prompt:
        assert tm == MOBA_BLOCK
        wuk, wuvt, slopes = extra_weights
        nblk = seq // MOBA_BLOCK
        in_specs += [_const_spec((KV_LORA, M_HEADS * MQ)), _const_spec((M_WIDTH, KV_LORA)),
                     pl.BlockSpec(memory_space=pltpu.SMEM)]
        args += [wuk, wuvt, slopes]
        bf_out = lambda n: jax.ShapeDtypeStruct((t, n), BF16)
        vt_shape = jax.ShapeDtypeStruct((batch, nblk, A_WIDTH, MOBA_BLOCK), BF16)
        vt_spec = pl.BlockSpec((None, 1, A_WIDTH, MOBA_BLOCK), lambda i: (i // n_pos_tiles, i % n_pos_tiles, 0, 0))
        out_shape += [bf_out(M_HEADS * MQ), bf_out(A_WIDTH), bf_out(A_WIDTH), vt_shape, bf_out(M_HEADS * MQ),
                      vt_shape, bf_out(A_HEADS * LANES)]
        out_specs += [pl.BlockSpec((tm, M_HEADS * MQ), row), pl.BlockSpec((tm, A_WIDTH), row),
                      pl.BlockSpec((tm, A_WIDTH), row), vt_spec, pl.BlockSpec((tm, M_HEADS * MQ), row), vt_spec,
                      pl.BlockSpec((tm, A_HEADS * LANES), row)]
        scratch_shapes = [pltpu.VMEM((nblk, A_WIDTH), F32)]
    else:
        assert batch == 1
        (wqa,) = extra_weights
        in_specs += [_const_spec((D_MODEL, A_WIDTH))]
        args += [wqa]
        out_shape += [f32_out(M_HEADS * MQ), f32_out(A_WIDTH), f32_out(A_WIDTH), f32_out(A_WIDTH), f32_out(M_ROPE),
                      jax.ShapeDtypeStruct((A_WIDTH, t), F32)]
        out_specs += [pl.BlockSpec((tm, M_HEADS * MQ), row), pl.BlockSpec((tm, A_WIDTH), row),
                      pl.BlockSpec((tm, A_WIDTH), row), pl.BlockSpec((tm, A_WIDTH), row),
                      pl.BlockSpec((tm, M_ROPE), row), pl.BlockSpec((A_WIDTH, tm), lambda i: (0, i))]
    return pl.pallas_call(
        functools.partial(_proj_kernel, prompt=prompt, tm=tm, nblk=seq // MOBA_BLOCK if prompt else None,
                          qk_scale=(M_NOPE + M_ROPE) ** -0.5 * (LOG2E if prompt else 1.0)),
        grid=(t // tm,),
        in_specs=in_specs,
        out_specs=out_specs,
        out_shape=out_shape,
        scratch_shapes=scratch_shapes,
        compiler_params=pltpu.CompilerParams(dimension_semantics=("arbitrary",), vmem_limit_bytes=VMEM_LIMIT),
        name="proj_prompt" if prompt else "proj_sample",
    )(*args)


ATTN_CHUNK = 2
CHUNKS_PER_STEP = 4


ONES_ROWS = 16


def _run_chains(carry, chains, logits_fn, values_fn):
    blk = MOBA_BLOCK
    carry = list(carry)
    ones = jnp.ones((ONES_ROWS, blk), BF16)
    scores = [logits_fn(*chain) for chain in chains]
    for (c, hh, _), logits in zip(chains, scores):
        m, acc = carry[hh]
        m_new = jnp.maximum(m, jnp.max(logits, axis=0, keepdims=True))
        pb = jnp.exp2(logits - m_new).astype(BF16)
        vt_parts = values_fn(c, hh)
        pv = None
        for t, vt in enumerate(vt_parts):
            part = _dot(jnp.concatenate([vt, ones], axis=0), pb[t * blk:(t + 1) * blk])
            pv = part if pv is None else pv + part
        carry[hh] = (m_new, jnp.exp2(m - m_new) * acc + pv)
    return tuple(carry)


def _attend_causal(qi, chunk, dv, logits_fn, values_fn):
    blk = MOBA_BLOCK
    heads = range(2)
    c_diag = qi // chunk
    n_past_steps = c_diag // CHUNKS_PER_STEP

    def step_chains(i, n_chunks, mask_last):
        return [(i * CHUNKS_PER_STEP + u, hh, mask_last and u == n_chunks - 1)
                for u in range(n_chunks) for hh in heads]

    init = tuple((jnp.full((1, blk), NEG, F32), jnp.zeros((dv + ONES_ROWS, blk), F32)) for _ in heads)
    last_step = [functools.partial(_run_chains, chains=step_chains(n_past_steps, n, True),
                                   logits_fn=logits_fn, values_fn=values_fn)
                 for n in range(1, CHUNKS_PER_STEP + 1)]
    carry = lax.switch(c_diag % CHUNKS_PER_STEP, last_step, init)
    carry = lax.fori_loop(
        0, n_past_steps,
        lambda i, cr: _run_chains(cr, step_chains(i, CHUNKS_PER_STEP, False), logits_fn, values_fn), carry)
    return jnp.concatenate([acc[:dv] / acc[dv:dv + 1] for _, acc in carry], axis=0)


def _chunk_key_minus_query(chunk):
    shape = (chunk * MOBA_BLOCK, MOBA_BLOCK)
    return lax.broadcasted_iota(jnp.int32, shape, 0) - lax.broadcasted_iota(jnp.int32, shape, 1)


def _top_blocks(gate, blk_id, n_cand):
    picks = []
    g = gate
    for _ in range(MOBA_TOPK):
        top = jnp.max(g, axis=0, keepdims=True)
        first = jnp.min(jnp.where(g == top, blk_id, n_cand), axis=0, keepdims=True)
        picks.append(first)
        g = jnp.where(blk_id == first, -jnp.inf, g)
    return picks


AUG_SLOPE_ROWS = SUBLANES


def _bf16_split3(x):
    hi = x.astype(BF16).astype(F32)
    mid = (x - hi).astype(BF16).astype(F32)
    lo = (x - hi - mid).astype(BF16).astype(F32)
    return hi, mid, lo


def _moba_key_bias_columns(seq):
    nblk = seq // MOBA_BLOCK
    assert AUG_SLOPE_ROWS + 3 * nblk <= LANES
    pos = jnp.arange(seq)
    col = jnp.arange(LANES)[None, :]
    in_blk = (col >= AUG_SLOPE_ROWS) & (col < AUG_SLOPE_ROWS + 3 * nblk)
    blk_of_col = (col - AUG_SLOPE_ROWS) % nblk
    aug = jnp.where(col < 3, (pos % MOBA_BLOCK)[:, None], 0)
    aug = jnp.where(in_blk & (blk_of_col == (pos // MOBA_BLOCK)[:, None]), 1, aug)
    return aug.astype(BF16)


def _moba_query_bias_rows(attended, slope2, qi, nblk):
    blk = MOBA_BLOCK
    blk_id = lax.broadcasted_iota(jnp.int32, (nblk, blk), 0)
    q_off = lax.broadcasted_iota(jnp.int32, (nblk, blk), 1).astype(F32)
    per_block = jnp.where(attended, 0.0, NEG) + slope2 * ((blk_id - qi) * blk).astype(F32) - slope2 * q_off
    row = lax.broadcasted_iota(jnp.int32, (AUG_SLOPE_ROWS, blk), 0)
    s_hi, s_mid, s_lo = _bf16_split3(jnp.full((AUG_SLOPE_ROWS, blk), slope2, F32))
    slope_rows = jnp.where(row == 0, s_hi, jnp.where(row == 1, s_mid, jnp.where(row == 2, s_lo, 0.0)))
    pad = jnp.zeros((LANES - AUG_SLOPE_ROWS - 3 * nblk, blk), F32)
    return jnp.concatenate([slope_rows, *_bf16_split3(per_block), pad], axis=0)


def _moba_prompt_gating(qa, slopes_ref, ksum_ref, qaug_o, *, nblk):
    blk = MOBA_BLOCK
    step = pl.program_id(0)
    qi = step % nblk

    @pl.when(step == 0)
    def _():
        ksum_ref[...] = jnp.zeros(ksum_ref.shape, F32)

    kmean = ksum_ref[...] * (1.0 / blk)
    blk_id = lax.broadcasted_iota(jnp.int32, (nblk, blk), 0)
    past = blk_id < qi
    lane = lax.broadcasted_iota(jnp.int32, (1, LANES), 1)
    for h in range(A_HEADS):
        pair = slice((h // 2) * LANES, (h // 2 + 1) * LANES)
        head_lanes = (lane >= A_HEAD_DIM) if h % 2 else (lane < A_HEAD_DIM)
        qf_h = jnp.where(head_lanes, qa[:, pair], 0.0)
        gate = _dot_nt(kmean[:, pair], qf_h, precision=lax.Precision.HIGHEST)
        attended = blk_id == qi
        for first in _top_blocks(jnp.where(past, gate, -jnp.inf), blk_id, nblk):
            attended = jnp.logical_or(attended, jnp.logical_and(blk_id == first, past))
        bias_rows = _moba_query_bias_rows(attended, slopes_ref[h] * LOG2E, qi, nblk)
        qaug_o[:, h * LANES:(h + 1) * LANES] = bias_rows.T.astype(BF16)


def _moba_prompt_kernel(qb_ref, qaug_ref, k_ref, kaug_ref, vt_ref, o_ref, *, chunk):
    blk = MOBA_BLOCK
    ck = chunk * blk
    qi = pl.program_id(2)
    lane = lax.broadcasted_iota(jnp.int32, (1, LANES), 1)
    heads = range(2)
    q = []
    for hh in heads:
        head_lanes = (lane >= A_HEAD_DIM) if hh else (lane < A_HEAD_DIM)
        q_h = jnp.where(head_lanes, qb_ref[...], jnp.zeros((), BF16))
        q.append(jnp.concatenate([q_h, qaug_ref[:, hh * LANES:(hh + 1) * LANES]], axis=1))

    rel = _chunk_key_minus_query(chunk)

    def logits(c, hh, masked):
        rows = pl.ds(pl.multiple_of(c * ck, ck), ck)
        keys = jnp.concatenate([k_ref[rows, :], kaug_ref[rows, :]], axis=1)
        s = _dot_nt(keys, q[hh])
        return jnp.where(rel <= (qi - c * chunk) * blk, s, NEG) if masked else s

    def values(c, hh):
        return [vt_ref[c * chunk + t, hh * A_HEAD_DIM:(hh + 1) * A_HEAD_DIM, :] for t in range(chunk)]

    o_ref[...] = _attend_causal(qi, chunk, A_HEAD_DIM, logits, values).T


def _moba_prompt(qab, qaug, kab, vatb, *, batch, seq):
    nblk = seq // MOBA_BLOCK
    blk = MOBA_BLOCK
    assert nblk % (ATTN_CHUNK * CHUNKS_PER_STEP) == 0
    q_spec = pl.BlockSpec((blk, LANES), lambda b, hp, qi: (b * nblk + qi, hp))
    return pl.pallas_call(
        functools.partial(_moba_prompt_kernel, chunk=ATTN_CHUNK),
        grid=(batch, HEAD_PAIRS, nblk),
        in_specs=[
            q_spec,
            pl.BlockSpec((blk, 2 * LANES), lambda b, hp, qi: (b * nblk + qi, hp)),
            pl.BlockSpec((seq, LANES), lambda b, hp, qi: (b, hp)),
            pl.BlockSpec((seq, LANES), lambda b, hp, qi: (0, 0)),
            pl.BlockSpec((None, nblk, LANES, blk), lambda b, hp, qi: (b, 0, hp, 0)),
        ],
        out_specs=q_spec,
        out_shape=jax.ShapeDtypeStruct((batch * seq, A_WIDTH), F32),
        compiler_params=pltpu.CompilerParams(dimension_semantics=("arbitrary",) * 3, vmem_limit_bytes=VMEM_LIMIT),
        name="moba_prompt",
    )(qab, qaug, kab, _moba_key_bias_columns(seq), vatb)


GATE_UNITS_PER_STEP = 2


def _mla_prompt_kernel(pt_ref, *refs, chunk, nblk, gate_steps, gate_nblk):
    n_gate_pages = GATE_UNITS_PER_STEP * PAGES_PER_STEP
    pages = refs[:n_gate_pages]
    (qat_ref, qs_ref, knew_ref, q_ref, k_ref, vt_ref, o_ref, sel_ref, score_ref, qb_ref,
     part_ref) = refs[n_gate_pages:]
    step = (pl.program_id(0) * HEAD_PAIRS + pl.program_id(1)) * nblk + pl.program_id(2)
    steps_per_seq = gate_steps // GATE_UNITS_PER_STEP
    for u in range(GATE_UNITS_PER_STEP):
        _moba_gate_unit(step // steps_per_seq, (step % steps_per_seq) * GATE_UNITS_PER_STEP + u,
                        pages[u * PAGES_PER_STEP:(u + 1) * PAGES_PER_STEP], u * PAGES_PER_STEP,
                        qat_ref, qs_ref, knew_ref, sel_ref, score_ref, qb_ref, part_ref,
                        n_steps=gate_steps, nblk=gate_nblk)

    blk = MOBA_BLOCK
    ck = chunk * blk
    qi = pl.program_id(2)
    heads = range(2)
    q = [q_ref[:, hh * MQ:(hh + 1) * MQ] for hh in heads]

    rel = _chunk_key_minus_query(chunk)

    def logits(c, hh, masked):
        rows = pl.ds(pl.multiple_of(c * ck, ck), ck)
        s = _dot_nt(k_ref[rows, hh * MQ:(hh + 1) * MQ], q[hh])
        return jnp.where(rel <= (qi - c * chunk) * blk, s, NEG) if masked else s

    def values(c, hh):
        return [vt_ref[c * chunk + t, hh * M_VDIM:(hh + 1) * M_VDIM, :] for t in range(chunk)]

    o_ref[...] = _attend_causal(qi, chunk, M_VDIM, logits, values).T


def _mla_prompt_and_gate(qm, km, vmtb, pt_flat, cache_kt, qat, q3, knew3, *, batch, seq, dec_batch, n_pages):
    nblk = seq // MOBA_BLOCK
    blk = MOBA_BLOCK
    gate_nblk = n_pages // PAGES_PER_BLOCK
    gate_steps = n_pages // PAGES_PER_STEP
    pages_per_grid_step = GATE_UNITS_PER_STEP * PAGES_PER_STEP
    steps_per_seq = gate_steps // GATE_UNITS_PER_STEP
    assert nblk % (ATTN_CHUNK * CHUNKS_PER_STEP) == 0
    assert n_pages % pages_per_grid_step == 0 and gate_nblk >= MOBA_TOPK and dec_batch <= LANES
    assert batch * HEAD_PAIRS * nblk == dec_batch * steps_per_seq

    def step_of(b, hp, qi):
        return (b * HEAD_PAIRS + hp) * nblk + qi

    def page_spec(i):
        def index_map(b, hp, qi, pt):
            t = step_of(b, hp, qi)
            page = (t % steps_per_seq) * pages_per_grid_step + i
            return (pt[page * dec_batch + t // steps_per_seq], 0, 0)
        return pl.BlockSpec((None, A_WIDTH, PAGE_SIZE), index_map)

    seq_vec = pl.BlockSpec((None, 1, A_WIDTH), lambda b, hp, qi, pt: (step_of(b, hp, qi) // steps_per_seq, 0, 0))
    grid_spec = pltpu.PrefetchScalarGridSpec(
        num_scalar_prefetch=1,
        grid=(batch, HEAD_PAIRS, nblk),
        in_specs=([page_spec(i) for i in range(pages_per_grid_step)] + [
            pl.BlockSpec((A_WIDTH, dec_batch), lambda b, hp, qi, pt: (0, 0)),
            seq_vec,
            seq_vec,
            pl.BlockSpec((blk, 2 * MQ), lambda b, hp, qi, pt: (b * nblk + qi, hp)),
            pl.BlockSpec((seq, 2 * MQ), lambda b, hp, qi, pt: (b, hp)),
            pl.BlockSpec((None, nblk, LANES, blk), lambda b, hp, qi, pt: (b, 0, hp, 0)),
        ]),
        out_specs=[
            pl.BlockSpec((blk, LANES), lambda b, hp, qi, pt: (b * nblk + qi, hp)),
            pl.BlockSpec((None, SUBLANES, LANES), lambda b, hp, qi, pt: (step_of(b, hp, qi) // steps_per_seq, 0, 0)),
            pl.BlockSpec((None, pages_per_grid_step, A_HEADS, PAGE_SIZE),
                         lambda b, hp, qi, pt: (step_of(b, hp, qi) // steps_per_seq,
                                                step_of(b, hp, qi) % steps_per_seq, 0, 0)),
        ],
        scratch_shapes=[pltpu.VMEM((A_WIDTH, LANES), F32), pltpu.VMEM((gate_nblk * A_HEADS, LANES), F32)],
    )
    return pl.pallas_call(
        functools.partial(_mla_prompt_kernel, chunk=ATTN_CHUNK, nblk=nblk, gate_steps=gate_steps,
                          gate_nblk=gate_nblk),
        grid_spec=grid_spec,
        out_shape=[jax.ShapeDtypeStruct((batch * seq, M_WIDTH), F32),
                   jax.ShapeDtypeStruct((dec_batch, SUBLANES, LANES), jnp.int32),
                   jax.ShapeDtypeStruct((dec_batch, n_pages, A_HEADS, PAGE_SIZE), F32)],
        compiler_params=pltpu.CompilerParams(dimension_semantics=("arbitrary",) * 3, vmem_limit_bytes=VMEM_LIMIT),
        name="mla_prompt_moba_gate",
    )(pt_flat, *([cache_kt] * pages_per_grid_step), qat, q3, knew3, qm, km, vmtb)


def _finish_kernel(x_ref, oa_ref, om_ref, goa_ref, gom_ref, wo_ref, gf_ref, wgu_ref, wd_ref, gfin_ref, y_ref):
    oa = _rms(oa_ref[...], goa_ref[...])
    om = _rms(om_ref[...], gom_ref[...])
    mix = jnp.concatenate([oa, om], axis=-1).astype(BF16)
    h = x_ref[...] + _dot(mix, wo_ref[...])
    hn = _rms(h, gf_ref[...]).astype(BF16)
    gu = _dot(hn, wgu_ref[...])
    g = gu[:, :D_FF]
    u = gu[:, D_FF:]
    act = (g / (1.0 + jnp.exp(-g)) * u).astype(BF16)
    h = h + _dot(act, wd_ref[...])
    y_ref[...] = _rms(h, gfin_ref[...])


def _finish(x2d, oa, om, g_oa, g_om, wo, g_ffn, wgu, wd, g_final, *, tm, name):
    t = x2d.shape[0]
    assert t % tm == 0
    row = lambda i: (i, 0)
    single = pl.Buffered(1)
    wspec = lambda shape: pl.BlockSpec(shape, lambda i: (0, 0), pipeline_mode=single)
    return pl.pallas_call(
        _finish_kernel,
        grid=(t // tm,),
        in_specs=[
            pl.BlockSpec((tm, D_MODEL), row),
            pl.BlockSpec((tm, A_WIDTH), row),
            pl.BlockSpec((tm, M_WIDTH), row),
            _const_spec((1, A_WIDTH)),
            _const_spec((1, M_WIDTH)),
            wspec((A_WIDTH + M_WIDTH, D_MODEL)),
            _const_spec((1, D_MODEL)),
            wspec((D_MODEL, 2 * D_FF)),
            wspec((D_FF, D_MODEL)),
            _const_spec((1, D_MODEL)),
        ],
        out_specs=pl.BlockSpec((tm, D_MODEL), row),
        out_shape=jax.ShapeDtypeStruct((t, D_MODEL), F32),
        compiler_params=pltpu.CompilerParams(dimension_semantics=("arbitrary",), vmem_limit_bytes=VMEM_LIMIT),
        name=name,
    )(x2d, oa, om, g_oa, g_om, wo, g_ffn, wgu, wd, g_final)


PAGES_PER_STEP = 16
PAGES_PER_BLOCK = MOBA_BLOCK // PAGE_SIZE
BLOCKS_PER_STEP = PAGES_PER_STEP // PAGES_PER_BLOCK


def _moba_gate_unit(b, s, pages, score_page0, qat_ref, q_ref, knew_ref, sel_ref, score_ref, qb_ref, part_ref, *,
                    n_steps, nblk):
    lane = lax.broadcasted_iota(jnp.int32, (1, LANES), 1)

    @pl.when(s == 0)
    def _():
        seq_id = lax.broadcasted_iota(jnp.int32, (1, qat_ref.shape[1]), 1)
        col = jnp.sum(jnp.where(seq_id == b, qat_ref[...], 0.0), axis=1, keepdims=True)
        qb_ref[...] = jnp.broadcast_to(col, qb_ref.shape)

    qb = qb_ref[...]
    for i in range(BLOCKS_PER_STEP):
        tiles = []
        for pg in range(PAGES_PER_BLOCK):
            prod = pages[PAGES_PER_BLOCK * i + pg][...] * qb
            tiles.append(jnp.concatenate(
                [jnp.sum(prod[h * A_HEAD_DIM:(h + 1) * A_HEAD_DIM], axis=0, keepdims=True)
                 for h in range(A_HEADS)], axis=0))
            score_ref[score_page0 + PAGES_PER_BLOCK * i + pg] = tiles[pg]
        part_ref[pl.ds(pl.multiple_of((s * BLOCKS_PER_STEP + i) * A_HEADS, A_HEADS), A_HEADS), :] = (
            functools.reduce(jnp.add, tiles))

    @pl.when(s == n_steps - 1)
    def _():
        gate = jnp.sum(part_ref[...], axis=1, keepdims=True) * (1.0 / MOBA_BLOCK)
        prod_new = knew_ref[...] * q_ref[...]
        lane_head = lax.broadcasted_iota(jnp.int32, (1, A_WIDTH), 1) // A_HEAD_DIM
        head_row = lax.broadcasted_iota(jnp.int32, (A_HEADS, 1), 0)
        g_new = jnp.zeros((A_HEADS, 1), F32)
        for h in range(A_HEADS):
            g_h = jnp.sum(jnp.where(lane_head == h, prod_new, 0.0), axis=1, keepdims=True)
            g_new = jnp.where(head_row == h, g_h * (1.0 / MOBA_BLOCK), g_new)
        n_cand = nblk + 1
        cand = [gate[j * A_HEADS:(j + 1) * A_HEADS] for j in range(nblk)] + [g_new]
        cand = [g if j < nblk else jnp.full_like(g, -jnp.inf) for j, g in enumerate(cand)]
        out = jnp.zeros((A_HEADS, LANES), jnp.int32)
        for r in range(MOBA_TOPK):
            top = functools.reduce(jnp.maximum, cand)
            first = functools.reduce(jnp.minimum, [jnp.where(g == top, j, n_cand) for j, g in enumerate(cand)])
            out = jnp.where(lane == r, first, out)
            cand = [jnp.where(first == j, -jnp.inf, g) for j, g in enumerate(cand)]
        sel_ref[...] = out


SLABS = 2 * MOBA_TOPK * PAGES_PER_BLOCK


def _moba_sample_kernel(pt_ref, sel_ref, slopes_ref, *refs, past_len):
    v_refs = refs[:SLABS]
    score_ref, q_ref, knew_ref, vnew_ref, o_ref = refs[SLABS:]
    b = pl.program_id(0)
    hp = pl.program_id(1)
    n_keys = MOBA_TOPK * MOBA_BLOCK
    slabs_per_head = MOBA_TOPK * PAGES_PER_BLOCK
    lane = lax.broadcasted_iota(jnp.int32, (1, n_keys), 1)
    scale = A_HEAD_DIM ** -0.5
    outs = []
    for hh in range(2):
        h = 2 * hp + hh
        cols = slice(hh * A_HEAD_DIM, (hh + 1) * A_HEAD_DIM)
        kpos = lane % MOBA_BLOCK
        pieces = []
        for r in range(MOBA_TOPK):
            blk = sel_ref[(b * A_HEADS + h) * MOBA_TOPK + r]
            kpos = kpos + jnp.where(lane // MOBA_BLOCK == r, blk * MOBA_BLOCK, 0)
            pieces += [score_ref[blk * PAGES_PER_BLOCK + pg, pl.ds(h, 1), :] for pg in range(PAGES_PER_BLOCK)]
        logits = jnp.concatenate(pieces, axis=1) * scale - slopes_ref[h] * (past_len - kpos).astype(F32)
        s_new = jnp.sum(knew_ref[:, cols] * q_ref[:, cols], axis=1, keepdims=True) * scale
        m = jnp.maximum(s_new, jnp.max(logits, axis=1, keepdims=True))
        p_new = jnp.exp(s_new - m)
        p = jnp.exp(logits - m)
        l = p_new + jnp.sum(p, axis=1, keepdims=True)
        vt = jnp.concatenate([v_refs[hh * slabs_per_head + i][...] for i in range(slabs_per_head)], axis=1)
        pv = _dot_nt(jnp.broadcast_to(p, (SUBLANES, n_keys)).astype(BF16), vt.astype(BF16))[:1]
        outs.append((p_new * vnew_ref[:, cols] + pv) / l)
    o_ref[...] = jnp.concatenate(outs, axis=1)


def _moba_sample(pt_flat, sel_flat, slopes, cache_vt, scores, q3, knew3, vnew3, *, dec_batch, n_pages):
    def slab_spec(slab):
        hh, rem = divmod(slab, MOBA_TOPK * PAGES_PER_BLOCK)
        r, pg = divmod(rem, PAGES_PER_BLOCK)

        def index_map(b, hp, pt, sel):
            blk = sel[(b * A_HEADS + 2 * hp + hh) * MOBA_TOPK + r]
            return (pt[(blk * PAGES_PER_BLOCK + pg) * dec_batch + b], 2 * hp + hh, 0, 0)

        return pl.BlockSpec((None, None, A_HEAD_DIM, PAGE_SIZE), index_map)

    vec_spec = pl.BlockSpec((None, 1, LANES), lambda b, hp, pt, sel: (b, 0, hp))
    slabs = [slab_spec(i) for i in range(SLABS)]
    grid_spec = pltpu.PrefetchScalarGridSpec(
        num_scalar_prefetch=2,
        grid=(dec_batch, HEAD_PAIRS),
        in_specs=([pl.BlockSpec(memory_space=pltpu.SMEM)] + slabs
                  + [pl.BlockSpec((None, n_pages, A_HEADS, PAGE_SIZE), lambda b, hp, pt, sel: (b, 0, 0, 0)),
                     vec_spec, vec_spec, vec_spec]),
        out_specs=vec_spec,
    )
    return pl.pallas_call(
        functools.partial(_moba_sample_kernel, past_len=n_pages * PAGE_SIZE),
        grid_spec=grid_spec,
        out_shape=jax.ShapeDtypeStruct((dec_batch, 1, A_WIDTH), F32),
        compiler_params=pltpu.CompilerParams(dimension_semantics=("arbitrary", "arbitrary"),
                                             vmem_limit_bytes=VMEM_LIMIT),
        name="moba_sample_attn",
    )(pt_flat, sel_flat, slopes, *([cache_vt] * SLABS), scores, q3, knew3, vnew3)


MLA_GROUPS = 4
MLA_GROUP_PAGES = 16
MLA_PAGES_PER_STEP = MLA_GROUPS * MLA_GROUP_PAGES


def _mla_sample_kernel(pt_ref, *refs, n_steps):
    ckv_pages = refs[:MLA_PAGES_PER_STEP]
    kpe_pages = refs[MLA_PAGES_PER_STEP:2 * MLA_PAGES_PER_STEP]
    (q_ref, ckvn_ref, kpen_ref, wuk_ref, wuv_ref, o_ref, qlat_ref, m_ref, l_ref,
     acc_ref) = refs[2 * MLA_PAGES_PER_STEP:]
    s = pl.program_id(1)
    q8 = q_ref[...]
    q_pe = q8[:, :M_ROPE]

    @pl.when(s == 0)
    def _():
        head_of_lane = lax.broadcasted_iota(jnp.int32, (M_HEADS, M_HEADS * MQ), 1) // MQ
        head_of_row = lax.broadcasted_iota(jnp.int32, (M_HEADS, M_HEADS * MQ), 0)
        q_bd = jnp.where(head_of_lane == head_of_row, jnp.concatenate([q8] * M_HEADS, axis=1), 0.0)
        qlat_ref[...] = _dot_nt(q_bd.astype(BF16), wuk_ref[...])
        m_ref[...] = jnp.full(m_ref.shape, NEG, F32)
        l_ref[...] = jnp.zeros(l_ref.shape, F32)
        acc_ref[...] = jnp.zeros(acc_ref.shape, F32)

    qlat = qlat_ref[...]
    qlat_b = qlat.astype(BF16)
    qpe_b = q_pe.astype(BF16)
    pcs, scores = [], []
    for g in range(MLA_GROUPS):
        pages = range(g * MLA_GROUP_PAGES, (g + 1) * MLA_GROUP_PAGES)
        pcs.append(jnp.concatenate([ckv_pages[i][...].astype(BF16) for i in pages], axis=0))
        kpt = jnp.concatenate([kpe_pages[i][...].astype(BF16) for i in pages], axis=1)
        scores.append(_dot_nt(qlat_b, pcs[g]) + _dot(qpe_b, kpt))
    for g in range(MLA_GROUPS):
        m_old = m_ref[g]
        m_new = jnp.maximum(m_old, jnp.max(scores[g], axis=1, keepdims=True))
        alpha = jnp.exp(m_old - m_new)
        p = jnp.exp(scores[g] - m_new)
        m_ref[g] = m_new
        l_ref[g] = alpha * l_ref[g] + jnp.sum(p, axis=1, keepdims=True)
        acc_ref[g] = alpha * acc_ref[g] + _dot(p.astype(BF16), pcs[g])

    @pl.when(s == n_steps - 1)
    def _():
        ckvn = ckvn_ref[...]
        s_new = (jnp.sum(qlat * ckvn, axis=1, keepdims=True)
                 + jnp.sum(q_pe * kpen_ref[...], axis=1, keepdims=True))
        m_fin = s_new
        for g in range(MLA_GROUPS):
            m_fin = jnp.maximum(m_fin, m_ref[g])
        p_new = jnp.exp(s_new - m_fin)
        num = p_new * ckvn
        den = p_new
        for g in range(MLA_GROUPS):
            a = jnp.exp(m_ref[g] - m_fin)
            num = num + a * acc_ref[g]
            den = den + a * l_ref[g]
        o_lat = num / den
        res = _dot(o_lat.astype(BF16), wuv_ref[...])
        head_of_lane = lax.broadcasted_iota(jnp.int32, (M_HEADS, M_WIDTH), 1) // M_VDIM
        head_of_row = lax.broadcasted_iota(jnp.int32, (M_HEADS, M_WIDTH), 0)
        o_ref[...] = jnp.sum(jnp.where(head_of_lane == head_of_row, res, 0.0), axis=0, keepdims=True)


def _mla_sample(pt_flat, cache_ckv, cache_kpet, q8, ckvn3, kpen3, wuk, wuv, *, dec_batch, n_pages):
    assert n_pages % MLA_PAGES_PER_STEP == 0
    n_steps = n_pages // MLA_PAGES_PER_STEP

    def page_spec(i, shape):
        return pl.BlockSpec((None,) + shape,
                            lambda b, s, pt: (pt[(s * MLA_PAGES_PER_STEP + i) * dec_batch + b], 0, 0))

    per_b = lambda shape: pl.BlockSpec((None,) + shape, lambda b, s, pt: (b, 0, 0))
    grid_spec = pltpu.PrefetchScalarGridSpec(
        num_scalar_prefetch=1,
        grid=(dec_batch, n_steps),
        in_specs=([page_spec(i, (PAGE_SIZE, KV_LORA)) for i in range(MLA_PAGES_PER_STEP)]
                  + [page_spec(i, (M_ROPE, PAGE_SIZE)) for i in range(MLA_PAGES_PER_STEP)]
                  + [per_b((M_HEADS, MQ)), per_b((1, KV_LORA)), per_b((1, M_ROPE)),
                     pl.BlockSpec((KV_LORA, M_HEADS * MQ), lambda b, s, pt: (0, 0)),
                     pl.BlockSpec((KV_LORA, M_WIDTH), lambda b, s, pt: (0, 0))]),
        out_specs=per_b((1, M_WIDTH)),
        scratch_shapes=[pltpu.VMEM((M_HEADS, KV_LORA), F32), pltpu.VMEM((MLA_GROUPS, M_HEADS, 1), F32),
                        pltpu.VMEM((MLA_GROUPS, M_HEADS, 1), F32), pltpu.VMEM((MLA_GROUPS, M_HEADS, KV_LORA), F32)],
    )
    return pl.pallas_call(
        functools.partial(_mla_sample_kernel, n_steps=n_steps),
        grid_spec=grid_spec,
        out_shape=jax.ShapeDtypeStruct((dec_batch, 1, M_WIDTH), F32),
        compiler_params=pltpu.CompilerParams(dimension_semantics=("arbitrary", "arbitrary"),
                                             vmem_limit_bytes=VMEM_LIMIT),
        name="mla_sample",
    )(pt_flat, *([cache_ckv] * MLA_PAGES_PER_STEP), *([cache_kpet] * MLA_PAGES_PER_STEP), q8, ckvn3, kpen3, wuk,
      wuv)


def _rot_half_cols(w):
    half = M_ROPE // 2
    return jnp.concatenate([-w[..., half:], w[..., :half]], axis=-1)


def _prep_weights(w_in, w_q_up, w_kv_up):
    kr = w_in[:, C_KR:C_KR + M_ROPE]
    pad = jnp.zeros((D_MODEL, LANES - M_ROPE), F32)
    w1 = jnp.concatenate([w_in[:, :C_KR], kr, pad, _rot_half_cols(kr), pad], axis=1).astype(BF16)
    wqa = w_in[:, C_QA:C_QA + A_WIDTH]
    wq3 = w_q_up.reshape(Q_LORA, M_HEADS, M_NOPE + M_ROPE)
    nope, rope = wq3[..., :M_NOPE], wq3[..., M_NOPE:]
    z = lambda n: jnp.zeros((Q_LORA, M_HEADS, n), F32)
    wq_a = jnp.concatenate([rope, nope, z(MQ - M_NOPE - M_ROPE)], axis=-1).reshape(Q_LORA, M_HEADS * MQ)
    wq_b = jnp.concatenate([_rot_half_cols(rope), z(MQ - M_ROPE)], axis=-1).reshape(Q_LORA, M_HEADS * MQ)
    wq = jnp.concatenate([wq_a, wq_b], axis=1).astype(BF16)
    wkv3 = w_kv_up.reshape(KV_LORA, M_HEADS, M_NOPE + M_VDIM)
    w_uk, w_uv = wkv3[..., :M_NOPE], wkv3[..., M_NOPE:]
    zk = lambda n: jnp.zeros((KV_LORA, M_HEADS, n), F32)
    wuk = jnp.concatenate([zk(M_ROPE), w_uk, zk(MQ - M_ROPE - M_NOPE)], axis=-1).reshape(KV_LORA, M_HEADS * MQ)
    wuv = w_uv.reshape(KV_LORA, M_WIDTH)
    return w1, wqa, wq, wuk.astype(BF16), wuv.astype(BF16), wuv.T.astype(BF16)


def _rope_tables(pos):
    half = M_ROPE // 2
    inv = ROPE_THETA ** (-jnp.arange(half, dtype=F32) / half)
    ang = pos.astype(F32)[:, None] * inv[None, :]
    n = pos.shape[0]
    cos = jnp.concatenate([jnp.cos(ang), jnp.cos(ang), jnp.ones((n, M_NOPE), F32),
                           jnp.zeros((n, MQ - M_ROPE - M_NOPE), F32)], axis=1)
    sin = jnp.concatenate([jnp.sin(ang), jnp.sin(ang), jnp.zeros((n, MQ - M_ROPE), F32)], axis=1)
    return cos, sin


def kernel(x_prompt, x_sample, cache_moba_k, cache_moba_v, cache_mla_ckv, cache_mla_kpe, page_table, g_attn, w_in,
           g_q, w_q_up, g_kv, w_kv_up, g_out_moba, g_out_mla, w_o, g_ffn, w_gate_up, w_down, g_final):
    batch, seq, _ = x_prompt.shape
    dec_batch, dec_seq, _ = x_sample.shape
    n_pool = cache_moba_k.shape[1]
    n_pages = page_table.shape[1]
    past_len = n_pages * PAGE_SIZE
    assert w_in.shape[0] == 1 and dec_seq == 1 and seq % MOBA_BLOCK == 0 and past_len % MOBA_BLOCK == 0

    w1, wqa, wq, wuk, wuv, wuvt = _prep_weights(w_in[0], w_q_up[0], w_kv_up[0])
    wo = w_o[0].astype(BF16)
    wgu = w_gate_up[0].astype(BF16)
    wd = w_down[0].astype(BF16)
    row2 = lambda a: a.reshape(1, -1)
    slopes = 2.0 ** (-8.0 * jnp.arange(1, A_HEADS + 1, dtype=F32) / A_HEADS)

    cos_p, sin_p = _rope_tables(jnp.arange(seq))
    xp = x_prompt.reshape(batch * seq, D_MODEL)
    (kat, vat, ckv, kpet, qm, qab, kab, vatb, km, vmtb, qaug) = _project(
        xp, cos_p, sin_p, row2(g_attn[0]), w1, row2(g_q[0]), wq, row2(g_kv[0]), (wuk, wuvt, slopes),
        prompt=True, tm=MOBA_BLOCK, seq=seq)
    cos_s, sin_s = _rope_tables(jnp.full((dec_batch,), past_len, jnp.int32))
    xs = x_sample.reshape(dec_batch, D_MODEL)
    (kat_s, vat_s, ckv_s, kpet_s, qm_s, qa_s, ka_s, va_s, kpe_s, qat_s) = _project(
        xs, cos_s, sin_s, row2(g_attn[0]), w1, row2(g_q[0]), wq, row2(g_kv[0]), (wqa,),
        prompt=False, tm=dec_batch, seq=dec_batch)
    pt_flat = page_table.T.reshape(-1)
    ckt = jnp.transpose(cache_moba_k[0], (0, 2, 3, 1))
    cvt = jnp.transpose(cache_moba_v[0], (0, 2, 3, 1))
    kpet_cache = jnp.transpose(cache_mla_kpe[0], (0, 2, 1))
    q3 = qa_s.reshape(dec_batch, 1, A_WIDTH)
    knew3 = ka_s.reshape(dec_batch, 1, A_WIDTH)
    vnew3 = va_s.reshape(dec_batch, 1, A_WIDTH)

    oa = _moba_prompt(qab, qaug, kab, vatb, batch=batch, seq=seq)
    om, sel, scores_s = _mla_prompt_and_gate(
        qm, km, vmtb, pt_flat, ckt.reshape(n_pool, A_WIDTH, PAGE_SIZE), qat_s, q3, knew3,
        batch=batch, seq=seq, dec_batch=dec_batch, n_pages=n_pages)
    finish_w = (row2(g_out_moba[0]), row2(g_out_mla[0]), wo, row2(g_ffn[0]), wgu, wd, row2(g_final))
    y_prompt = _finish(xp, oa, om, *finish_w, tm=512, name="finish_prompt").reshape(batch, seq, D_MODEL)

    oa_s = _moba_sample(pt_flat, sel[:, :A_HEADS, :MOBA_TOPK].reshape(-1), slopes, cvt, scores_s, q3, knew3, vnew3,
                        dec_batch=dec_batch, n_pages=n_pages)
    om_s = _mla_sample(pt_flat, cache_mla_ckv[0], kpet_cache, qm_s.reshape(dec_batch, M_HEADS, MQ),
                       ckv_s.reshape(dec_batch, 1, KV_LORA), kpe_s.reshape(dec_batch, 1, M_ROPE), wuk, wuv,
                       dec_batch=dec_batch, n_pages=n_pages)
    y_sample = _finish(xs, oa_s.reshape(dec_batch, A_WIDTH), om_s.reshape(dec_batch, M_WIDTH), *finish_w,
                       tm=dec_batch, name="finish_sample").reshape(dec_batch, 1, D_MODEL)

    def heads_last(t_minor, n_tok_batch, n_tok):
        return jnp.transpose(t_minor.reshape(1, n_tok_batch, A_HEADS, A_HEAD_DIM, n_tok), (0, 1, 4, 2, 3))

    k_sample = jnp.transpose(kat_s.reshape(1, 1, A_HEADS, A_HEAD_DIM, dec_batch), (0, 4, 1, 2, 3))
    v_sample = jnp.transpose(vat_s.reshape(1, 1, A_HEADS, A_HEAD_DIM, dec_batch), (0, 4, 1, 2, 3))
    return (y_prompt, y_sample,
            heads_last(kat, batch, seq), heads_last(vat, batch, seq),
            ckv.reshape(1, batch, seq, KV_LORA), jnp.transpose(kpet.reshape(1, batch, M_ROPE, seq), (0, 1, 3, 2)),
            k_sample, v_sample,
            ckv_s.reshape(1, dec_batch, 1, KV_LORA),
            jnp.transpose(kpet_s.reshape(1, 1, M_ROPE, dec_batch), (0, 3, 1, 2)))
```
